```python
import jax, jax.numpy as jnp
from jax import lax
import numpy as np

D_MODEL = 2048
BATCH = 32
SEQ = 256
DEPTH = 4
DEC_BATCH = 8
DEC_SEQ = 1024
PAST_LEN = 512

GRID_W = 64
HEAD_DIM = 128
CONV_CH = 512
CONV_WIDTH = 31
GLOB_HEADS = 6
GLOB_KV = 2
WIN_HEADS = 6
WIN_KV = 2
WINDOW = 128
Q_BLOCK = 128
D_FF = 4 * D_MODEL
ROPE_THETA = 10000.0
NORM_EPS = 1e-6
NEG_INF = -1e30

GLOB_Q = GLOB_HEADS * HEAD_DIM
GLOB_KVW = GLOB_KV * HEAD_DIM
WIN_Q = WIN_HEADS * HEAD_DIM
WIN_KVW = WIN_KV * HEAD_DIM
MIX_WIDTH = CONV_CH + GLOB_Q + WIN_Q
IN_WIDTH = 2 * CONV_CH + GLOB_Q + 2 * GLOB_KVW + WIN_Q + 2 * WIN_KVW
IN_SPLITS = tuple(int(v) for v in np.cumsum(
    [2 * CONV_CH, GLOB_Q, GLOB_KVW, GLOB_KVW, WIN_Q, WIN_KVW])[:])

kernel_name = 'hybrid_dit_conv_gqa_swa_step'


def rms_norm(x, g):
    xf = x.astype(jnp.float32)
    y = xf * lax.rsqrt(jnp.mean(xf * xf, axis=-1, keepdims=True) + NORM_EPS)
    return (y * g.astype(jnp.float32)).astype(x.dtype)


def layer_norm(x, g, b):
    xf = x.astype(jnp.float32)
    mu = jnp.mean(xf, axis=-1, keepdims=True)
    var = jnp.mean(jnp.square(xf - mu), axis=-1, keepdims=True)
    y = (xf - mu) * lax.rsqrt(var + NORM_EPS)
    return (y * g.astype(jnp.float32) + b.astype(jnp.float32)).astype(x.dtype)


def axial_rope_tables(n_tokens):
    rows = n_tokens // GRID_W
    r, col = jnp.meshgrid(jnp.arange(rows), jnp.arange(GRID_W), indexing='ij')
    r = r.reshape(-1).astype(jnp.float32)
    col = col.reshape(-1).astype(jnp.float32)
    n_freq = HEAD_DIM // 4
    inv = ROPE_THETA ** (-jnp.arange(n_freq, dtype=jnp.float32) / n_freq)
    ang_r = r[:, None] * inv
    ang_c = col[:, None] * inv
    ang = jnp.concatenate([ang_r, ang_r, ang_c, ang_c], axis=-1)
    return jnp.cos(ang), jnp.sin(ang)


def apply_rope(x, cos, sin):
    x1, x2, x3, x4 = jnp.split(x, 4, axis=-1)
    rot = jnp.concatenate([-x2, x1, -x4, x3], axis=-1)
    shape = (cos.shape[0],) + (1,) * (x.ndim - 3) + (HEAD_DIM,)
    cos = cos.reshape(shape)
    sin = sin.reshape(shape)
    return (x.astype(jnp.float32) * cos + rot.astype(jnp.float32) * sin).astype(x.dtype)


def softmax_with_sink(s, sink):
    if sink is None:
        return jax.nn.softmax(s, axis=-1)
    snk = sink.astype(jnp.float32)[None, :, :, None, None]
    m = jnp.maximum(jnp.max(s, axis=-1, keepdims=True), snk)
    e = jnp.exp(s - m)
    return e / (jnp.sum(e, axis=-1, keepdims=True) + jnp.exp(snk - m))


def block_attention(q, k, v, sink):
    B, S, KV, G, D = q.shape
    nb = S // Q_BLOCK
    scale = HEAD_DIM ** -0.5
    qb = jnp.moveaxis(q.reshape(B, nb, Q_BLOCK, KV, G, D), 1, 0)

    def one(qi):
        s = jnp.einsum('bqkgd,bskd->bkgqs', qi, k).astype(jnp.float32) * scale
        p = softmax_with_sink(s, sink).astype(v.dtype)
        return jnp.einsum('bkgqs,bskd->bqkgd', p, v)

    o = lax.map(one, qb)
    return jnp.moveaxis(o, 0, 1).reshape(B, S, KV * G * D)


def banded_attention_with_context(q, k, v, k_ctx, v_ctx, sink):
    B, L, KV, G, D = q.shape
    nb = L // Q_BLOCK
    span = Q_BLOCK + 2 * WINDOW
    n_ctx = k_ctx.shape[1]
    scale = HEAD_DIM ** -0.5
    pad = ((0, 0), (WINDOW, WINDOW), (0, 0), (0, 0))
    k_pad = jnp.pad(k, pad)
    v_pad = jnp.pad(v, pad)
    qb = jnp.moveaxis(q.reshape(B, nb, Q_BLOCK, KV, G, D), 1, 0)

    def one(args):
        qi, bi = args
        start = bi * Q_BLOCK
        kb = lax.dynamic_slice_in_dim(k_pad, start, span, axis=1)
        vb = lax.dynamic_slice_in_dim(v_pad, start, span, axis=1)
        qpos = start + jnp.arange(Q_BLOCK)
        kpos = start - WINDOW + jnp.arange(span)
        valid = ((jnp.abs(qpos[:, None] - kpos[None, :]) <= WINDOW)
                 & (kpos >= 0)[None, :] & (kpos < L)[None, :])
        s_loc = jnp.einsum('bqkgd,bskd->bkgqs', qi, kb).astype(jnp.float32) * scale
        s_loc = jnp.where(valid, s_loc, NEG_INF)
        s_ctx = jnp.einsum('bqkgd,bskd->bkgqs', qi, k_ctx).astype(jnp.float32) * scale
        p = softmax_with_sink(jnp.concatenate([s_ctx, s_loc], axis=-1), sink).astype(v.dtype)
        return (jnp.einsum('bkgqs,bskd->bqkgd', p[..., :n_ctx], v_ctx)
                + jnp.einsum('bkgqs,bskd->bqkgd', p[..., n_ctx:], vb))

    o = lax.map(one, (qb, jnp.arange(nb)))
    return jnp.moveaxis(o, 0, 1).reshape(B, L, KV * G * D)


def conv_module(u, w, b, ln_g, ln_b):
    a, gt = jnp.split(u, 2, axis=-1)
    h = a * jax.nn.sigmoid(gt)
    y = lax.conv_general_dilated(
        h, w.reshape(CONV_WIDTH, 1, CONV_CH).astype(h.dtype),
        window_strides=(1,), padding=[(CONV_WIDTH // 2, CONV_WIDTH // 2)],
        dimension_numbers=('NWC', 'WIO', 'NWC'), feature_group_count=CONV_CH)
    y = layer_norm(y + b, ln_g, ln_b)
    return jax.nn.silu(y)


def trunk_layer(x, mod, lp, ctx_kv=None, rope=None):
    shift1, scale1, gate1, shift2, scale2, gate2 = jnp.split(mod, 6, axis=-1)
    B, S, _ = x.shape
    h = rms_norm(x, lp['g_attn']) * (1 + scale1) + shift1
    proj = h @ lp['w_in']
    u_conv, q_g, k_g, v_g, q_w, k_w, v_w = jnp.split(proj, IN_SPLITS, axis=-1)
    q_g = rms_norm(q_g.reshape(B, S, GLOB_KV, GLOB_HEADS // GLOB_KV, HEAD_DIM), lp['q_norm'])
    k_g = rms_norm(k_g.reshape(B, S, GLOB_KV, HEAD_DIM), lp['k_norm'])
    v_g = v_g.reshape(B, S, GLOB_KV, HEAD_DIM)
    q_w = q_w.reshape(B, S, WIN_KV, WIN_HEADS // WIN_KV, HEAD_DIM)
    k_w = k_w.reshape(B, S, WIN_KV, HEAD_DIM)
    v_w = v_w.reshape(B, S, WIN_KV, HEAD_DIM)
    sink = lp['sink'].reshape(WIN_KV, WIN_HEADS // WIN_KV)

    y_conv = conv_module(u_conv, lp['conv_w'], lp['conv_b'], lp['conv_ln_g'], lp['conv_ln_b'])
    if ctx_kv is None:
        y_g = block_attention(q_g, k_g, v_g, None)
        y_w = block_attention(q_w, k_w, v_w, sink)
        new_kv = (k_g, v_g, k_w, v_w)
    else:
        cos, sin = rope
        ck_g, cv_g, ck_w, cv_w = ctx_kv
        q_g = apply_rope(q_g, cos, sin)
        k_g = apply_rope(k_g, cos, sin)
        y_g = block_attention(q_g, jnp.concatenate([ck_g, k_g], axis=1),
                              jnp.concatenate([cv_g, v_g], axis=1), None)
        q_w = apply_rope(q_w, cos, sin)
        k_w = apply_rope(k_w, cos, sin)
        y_w = banded_attention_with_context(q_w, k_w, v_w, ck_w, cv_w, sink)
        new_kv = None
    mix = jnp.concatenate([y_conv, y_g, y_w], axis=-1) @ lp['w_out']
    x = x + gate1 * mix
    h2 = rms_norm(x, lp['g_mlp']) * (1 + scale2) + shift2
    x = x + gate2 * (jnp.square(jax.nn.relu(h2 @ lp['w_mlp1'])) @ lp['w_mlp2'])
    return x, new_kv


def setup_inputs(seed: int = 0) -> dict:
    key = jax.random.key(seed)
    ks = jax.random.split(key, 32)
    f32 = jnp.float32
    n = lambda i, shape, s: jax.random.normal(ks[i], shape, f32) * s
    cache_g = (DEC_BATCH, DEPTH, PAST_LEN, GLOB_KV, HEAD_DIM)
    cache_w = (DEC_BATCH, DEPTH, PAST_LEN, WIN_KV, HEAD_DIM)
    return {
        'x_prompt': n(0, (BATCH, SEQ, D_MODEL), 1.0),
        'x_sample': n(1, (DEC_BATCH, DEC_SEQ, D_MODEL), 1.0),
        'cache_glob_k': n(2, cache_g, 1.0),
        'cache_glob_v': n(3, cache_g, 1.0),
        'cache_win_k': n(4, cache_w, 1.0),
        'cache_win_v': n(5, cache_w, 1.0),
        'c': n(6, (DEC_BATCH, D_MODEL), 1.0),
        'c_ctx': n(7, (D_MODEL,), 1.0),
        'w_ada': n(8, (DEPTH, D_MODEL, 6 * D_MODEL), 0.5 * D_MODEL ** -0.5),
        'b_ada': n(9, (DEPTH, 6 * D_MODEL), 0.02),
        'g_attn': 1.0 + n(10, (DEPTH, D_MODEL), 0.02),
        'g_mlp': 1.0 + n(11, (DEPTH, D_MODEL), 0.02),
        'w_in': n(12, (DEPTH, D_MODEL, IN_WIDTH), D_MODEL ** -0.5),
        'conv_w': n(13, (DEPTH, CONV_WIDTH, CONV_CH), CONV_WIDTH ** -0.5),
        'conv_b': n(14, (DEPTH, CONV_CH), 0.02),
        'conv_ln_g': 1.0 + n(15, (DEPTH, CONV_CH), 0.02),
        'conv_ln_b': n(16, (DEPTH, CONV_CH), 0.02),
        'q_norm_g': 1.0 + n(17, (DEPTH, HEAD_DIM), 0.02),
        'k_norm_g': 1.0 + n(18, (DEPTH, HEAD_DIM), 0.02),
        'sink': n(19, (DEPTH, WIN_HEADS), 0.5),
        'w_out': n(20, (DEPTH, MIX_WIDTH, D_MODEL), MIX_WIDTH ** -0.5),
        'w_mlp1': n(21, (DEPTH, D_MODEL, D_FF), D_MODEL ** -0.5),
        'w_mlp2': n(22, (DEPTH, D_FF, D_MODEL), D_FF ** -0.5),
        'g_final': 1.0 + n(23, (D_MODEL,), 0.02),
    }


def reference(x_prompt, x_sample, cache_glob_k, cache_glob_v, cache_win_k, cache_win_v,
              c, c_ctx, w_ada, b_ada, g_attn, g_mlp, w_in, conv_w, conv_b, conv_ln_g,
              conv_ln_b, q_norm_g, k_norm_g, sink, w_out, w_mlp1, w_mlp2, g_final):
    rope = axial_rope_tables(x_sample.shape[1])
    xp, xs = x_prompt, x_sample
    gk, gv, wk, wv = [], [], [], []
    for l in range(DEPTH):
        lp = {'g_attn': g_attn[l], 'g_mlp': g_mlp[l], 'w_in': w_in[l], 'conv_w': conv_w[l],
              'conv_b': conv_b[l], 'conv_ln_g': conv_ln_g[l], 'conv_ln_b': conv_ln_b[l],
              'q_norm': q_norm_g[l], 'k_norm': k_norm_g[l], 'sink': sink[l],
              'w_out': w_out[l], 'w_mlp1': w_mlp1[l], 'w_mlp2': w_mlp2[l]}
        mod_ctx = (jax.nn.silu(c_ctx) @ w_ada[l] + b_ada[l])[None, None, :]
        mod_lat = (jax.nn.silu(c) @ w_ada[l] + b_ada[l])[:, None, :]
        xp, kv = trunk_layer(xp, mod_ctx, lp)
        gk.append(kv[0]); gv.append(kv[1]); wk.append(kv[2]); wv.append(kv[3])
        ctx_kv = (cache_glob_k[:, l], cache_glob_v[:, l], cache_win_k[:, l], cache_win_v[:, l])
        xs, _ = trunk_layer(xs, mod_lat, lp, ctx_kv=ctx_kv, rope=rope)
    y_prompt = rms_norm(xp, g_final)
    y_sample = rms_norm(xs, g_final)
    new_glob_k = jnp.stack(gk, axis=1)
    new_glob_v = jnp.stack(gv, axis=1)
    new_win_k = jnp.stack(wk, axis=1)
    new_win_v = jnp.stack(wv, axis=1)
    return (y_prompt, y_sample, new_glob_k, new_glob_v, new_win_k, new_win_v)
```

```python
import functools

import jax
import jax.numpy as jnp
import numpy as np
from jax import lax
from jax.experimental import pallas as pl
from jax.experimental.pallas import tpu as pltpu

HEAD_DIM = 128
CONV_CH = 512
CONV_WIDTH = 31
GLOB_HEADS = 6
GLOB_KV = 2
WIN_HEADS = 6
WIN_KV = 2
WINDOW = 128
GRID_W = 64
ROPE_THETA = 10000.0
NORM_EPS = 1e-6
NEG_INF = -1e30

GLOB_Q = GLOB_HEADS * HEAD_DIM
GLOB_KVW = GLOB_KV * HEAD_DIM
WIN_Q = WIN_HEADS * HEAD_DIM
WIN_KVW = WIN_KV * HEAD_DIM
GROUP = GLOB_HEADS // GLOB_KV
C_U = 0
C_QG = 2 * CONV_CH
C_KG = C_QG + GLOB_Q
C_VG = C_KG + GLOB_KVW
C_QW = C_VG + GLOB_KVW
C_KW = C_QW + WIN_Q
C_VW = C_KW + WIN_KVW
IN_WIDTH = C_VW + WIN_KVW

MOD_ROWS = 16
CONV_HALO = 16
CONV_BLOCK = 256
V7X_VMEM_LIMIT = 56 * 1024 * 1024

BF16 = jnp.bfloat16
F32 = jnp.float32


def _params(*sem):
    return pltpu.CompilerParams(dimension_semantics=sem, vmem_limit_bytes=V7X_VMEM_LIMIT)


def _sigmoid(x):
    return 1.0 / (1.0 + jnp.exp(-x))


def _rms(x, gain):
    return x * lax.rsqrt(jnp.mean(x * x, axis=-1, keepdims=True) + NORM_EPS) * gain


def _ada_kernel(c_ref, w_ref, b_ref, o_ref):
    c = c_ref[...]
    a = (c * _sigmoid(c)).astype(BF16)
    o_ref[...] = jnp.dot(a, w_ref[...].astype(BF16), preferred_element_type=F32) + b_ref[...]


def _ada(cin, w_ada, b_ada):
    depth, d, n = w_ada.shape
    tn = 1024
    return pl.pallas_call(
        _ada_kernel,
        grid=(depth, n // tn),
        in_specs=[
            pl.BlockSpec((MOD_ROWS, d), lambda l, j: (0, 0)),
            pl.BlockSpec((None, d, tn), lambda l, j: (l, 0, j)),
            pl.BlockSpec((None, 1, tn), lambda l, j: (l, 0, j)),
        ],
        out_specs=pl.BlockSpec((None, MOD_ROWS, tn), lambda l, j: (l, 0, j)),
        out_shape=jax.ShapeDtypeStruct((depth, MOD_ROWS, n), F32),
        compiler_params=_params("arbitrary", "arbitrary"),
        name="ada_mod",
    )(cin, w_ada, b_ada.reshape(depth, 1, n))


def _inproj_kernel(*refs, rope, emit_cache):
    x_ref, mod_ref, g_ref, w_ref, qn_ref, kn_ref = refs[:6]
    pos = 6
    if rope:
        cos_ref, sin_ref = refs[pos:pos + 2]
        pos += 2
    u_ref, qg_ref, kg_ref, vg_ref, qw_ref, kw_ref, vw_ref = refs[pos:pos + 7]
    pos += 7
    if emit_cache:
        kgc_ref, vgc_ref, kwc_ref, vwc_ref = refs[pos:pos + 4]

    x = x_ref[...]
    h = _rms(x, g_ref[...]) * (1.0 + mod_ref[1:2, :]) + mod_ref[0:1, :]
    hb = h.astype(BF16)

    def proj(c0, c1):
        return jnp.dot(hb, w_ref[:, c0:c1], preferred_element_type=F32)

    if rope:
        cos = cos_ref[...]
        sin = sin_ref[...]
        lane = lax.broadcasted_iota(jnp.int32, cos.shape, 1)
        take_next = ((lane // (HEAD_DIM // 4)) % 2) == 0

        def rot(t):
            r = jnp.where(take_next, pltpu.roll(t, HEAD_DIM - HEAD_DIM // 4, 1),
                          pltpu.roll(t, HEAD_DIM // 4, 1))
            return t * cos + r * sin
    else:
        def rot(t):
            return t

    def heads(p, n):
        return [p[:, i * HEAD_DIM:(i + 1) * HEAD_DIM] for i in range(n)]

    u_ref[...] = proj(C_U, C_QG)

    qn = qn_ref[...]
    kn = kn_ref[...]
    for i, t in enumerate(heads(proj(C_QG, C_KG), GLOB_HEADS)):
        qg_ref[:, i * HEAD_DIM:(i + 1) * HEAD_DIM] = rot(_rms(t, qn)).astype(BF16)
    for i, t in enumerate(heads(proj(C_KG, C_VG), GLOB_KV)):
        t = _rms(t, kn)
        if emit_cache:
            kgc_ref[:, i * HEAD_DIM:(i + 1) * HEAD_DIM] = t
        kg_ref[:, i * HEAD_DIM:(i + 1) * HEAD_DIM] = rot(t).astype(BF16)
    p = proj(C_VG, C_QW)
    if emit_cache:
        vgc_ref[...] = p
    vg_ref[...] = p.astype(BF16)
    for i, t in enumerate(heads(proj(C_QW, C_KW), WIN_HEADS)):
        qw_ref[:, i * HEAD_DIM:(i + 1) * HEAD_DIM] = rot(t).astype(BF16)
    p = proj(C_KW, C_VW)
    if emit_cache:
        kwc_ref[...] = p
    for i, t in enumerate(heads(p, WIN_KV)):
        kw_ref[:, i * HEAD_DIM:(i + 1) * HEAD_DIM] = rot(t).astype(BF16)
    p = proj(C_VW, IN_WIDTH)
    if emit_cache:
        vwc_ref[...] = p
    vw_ref[...] = p.astype(BF16)


def _inproj(x, mod, row_of_tile, g, w, layer, qn, kn, rope_tabs, seq, emit_cache):
    t_tokens, d = x.shape
    tm = 256
    rope = rope_tabs is not None
    const = lambda i: (0, 0)
    in_specs = [
        pl.BlockSpec((tm, d), lambda i: (i, 0)),
        pl.BlockSpec((None, 6, d), lambda i: (row_of_tile(i, tm), 0, 0)),
        pl.BlockSpec((1, d), const),
        pl.BlockSpec((None, d, IN_WIDTH), lambda i: (layer, 0, 0)),
        pl.BlockSpec((1, HEAD_DIM), const),
        pl.BlockSpec((1, HEAD_DIM), const),
    ]
    args = [x, mod, g, w, qn, kn]
    if rope:
        per_seq = seq // tm
        in_specs += [pl.BlockSpec((tm, HEAD_DIM), lambda i: (i % per_seq, 0))] * 2
        args += list(rope_tabs)
    widths = [(C_QG - C_U, F32), (GLOB_Q, BF16), (GLOB_KVW, BF16), (GLOB_KVW, BF16),
              (WIN_Q, BF16), (WIN_KVW, BF16), (WIN_KVW, BF16)]
    if emit_cache:
        widths += [(GLOB_KVW, F32), (GLOB_KVW, F32), (WIN_KVW, F32), (WIN_KVW, F32)]
    out_specs = [pl.BlockSpec((tm, wd), lambda i: (i, 0)) for wd, _ in widths]
    out_shape = [jax.ShapeDtypeStruct((t_tokens, wd), dt) for wd, dt in widths]
    return pl.pallas_call(
        functools.partial(_inproj_kernel, rope=rope, emit_cache=emit_cache),
        grid=(t_tokens // tm,),
        in_specs=in_specs,
        out_specs=out_specs,
        out_shape=out_shape,
        compiler_params=_params("arbitrary"),
        name="inproj_rope" if rope else "inproj_ctx",
    )(*args)


def _conv_kernel(*refs, blocks_per_seq):
    halo = blocks_per_seq > 1
    if halo:
        u_ref, up_ref, un_ref, w_ref, b_ref, lg_ref, lb_ref, o_ref, hp_ref, y_ref = refs
    else:
        u_ref, w_ref, b_ref, lg_ref, lb_ref, o_ref, hp_ref, y_ref = refs
    rows = u_ref.shape[0]

    def glu(u):
        return u[:, :CONV_CH] * _sigmoid(u[:, CONV_CH:])

    zeros = jnp.zeros((CONV_HALO, CONV_CH), F32)
    if halo:
        j = pl.program_id(0) % blocks_per_seq
        hp_ref[0:CONV_HALO, :] = jnp.where(j > 0, glu(up_ref[...]), zeros)
        hp_ref[CONV_HALO + rows:, :] = jnp.where(j < blocks_per_seq - 1, glu(un_ref[...]), zeros)
    else:
        hp_ref[0:CONV_HALO, :] = zeros
        hp_ref[CONV_HALO + rows:, :] = zeros
    hp_ref[CONV_HALO:CONV_HALO + rows, :] = glu(u_ref[...])

    rchunk = 64
    base = CONV_HALO - CONV_WIDTH // 2
    for c in range(CONV_CH // HEAD_DIM):
        cs = slice(c * HEAD_DIM, (c + 1) * HEAD_DIM)
        for r in range(rows // rchunk):
            acc = jnp.zeros((rchunk, HEAD_DIM), F32)
            for k in range(CONV_WIDTH):
                r0 = r * rchunk + base + k
                acc = acc + hp_ref[r0:r0 + rchunk, cs] * w_ref[k:k + 1, cs]
            y_ref[r * rchunk:(r + 1) * rchunk, cs] = acc

    y = y_ref[...] + b_ref[...]
    mu = jnp.mean(y, axis=-1, keepdims=True)
    yc = y - mu
    var = jnp.mean(yc * yc, axis=-1, keepdims=True)
    z = yc * lax.rsqrt(var + NORM_EPS) * lg_ref[...] + lb_ref[...]
    o_ref[...] = (z * _sigmoid(z)).astype(BF16)


def _conv(u, w, b, lg, lb, seq):
    t_tokens = u.shape[0]
    rows = CONV_BLOCK
    blocks_per_seq = seq // rows
    halo_per_block = rows // CONV_HALO
    n_halo = t_tokens // CONV_HALO
    const = lambda i: (0, 0)
    in_specs = [pl.BlockSpec((rows, 2 * CONV_CH), lambda i: (i, 0))]
    args = [u]
    if blocks_per_seq > 1:
        in_specs += [
            pl.BlockSpec((CONV_HALO, 2 * CONV_CH), lambda i: (jnp.maximum(i * halo_per_block - 1, 0), 0)),
            pl.BlockSpec((CONV_HALO, 2 * CONV_CH),
                         lambda i: (jnp.minimum((i + 1) * halo_per_block, n_halo - 1), 0)),
        ]
        args += [u, u]
    in_specs += [pl.BlockSpec((CONV_WIDTH, CONV_CH), const)] + [pl.BlockSpec((1, CONV_CH), const)] * 3
    args += [w, b, lg, lb]
    return pl.pallas_call(
        functools.partial(_conv_kernel, blocks_per_seq=blocks_per_seq),
        grid=(t_tokens // rows,),
        in_specs=in_specs,
        out_specs=pl.BlockSpec((rows, CONV_CH), lambda i: (i, 0)),
        out_shape=jax.ShapeDtypeStruct((t_tokens, CONV_CH), BF16),
        scratch_shapes=[pltpu.VMEM((rows + 2 * CONV_HALO, CONV_CH), F32),
                        pltpu.VMEM((rows, CONV_CH), F32)],
        compiler_params=_params("arbitrary"),
        name="conv_mixer",
    )(*args)


def _stack_heads(q, g):
    return jnp.concatenate(
        [q[:, (g * GROUP + j) * HEAD_DIM:(g * GROUP + j + 1) * HEAD_DIM] for j in range(GROUP)], axis=0)


def _scores(q, k):
    s = lax.dot_general(q, k, (((1,), (1,)), ((), ())), preferred_element_type=F32)
    return s * (HEAD_DIM ** -0.5)


def _sink_column(sink_ref, g, rows):
    return jnp.concatenate(
        [jnp.full((rows, 1), sink_ref[g * GROUP + j], F32) for j in range(GROUP)], axis=0)


def _store_heads(o_ref, o, g, rows):
    for j in range(GROUP):
        c0 = (g * GROUP + j) * HEAD_DIM
        o_ref[:, c0:c0 + HEAD_DIM] = o[j * rows:(j + 1) * rows].astype(o_ref.dtype)


def _attn_ctx_kernel(sink_ref, qg_ref, kg_ref, vg_ref, qw_ref, kw_ref, vw_ref, og_ref, ow_ref, *, seq):
    nb = qg_ref.shape[0] // seq
    for b in range(nb):
        rs = slice(b * seq, (b + 1) * seq)
        for g in range(GLOB_KV):
            hs = slice(g * HEAD_DIM, (g + 1) * HEAD_DIM)
            s = _scores(_stack_heads(qg_ref[rs, :], g), kg_ref[rs, hs])
            e = jnp.exp(s - jnp.max(s, axis=-1, keepdims=True))
            p = e * (1.0 / jnp.sum(e, axis=-1, keepdims=True))
            o = jnp.dot(p.astype(BF16), vg_ref[rs, hs], preferred_element_type=F32)
            _store_heads(og_ref.at[rs, :], o, g, seq)
            s = _scores(_stack_heads(qw_ref[rs, :], g), kw_ref[rs, hs])
            snk = _sink_column(sink_ref, g, seq)
            m = jnp.maximum(jnp.max(s, axis=-1, keepdims=True), snk)
            e = jnp.exp(s - m)
            p = e * (1.0 / (jnp.sum(e, axis=-1, keepdims=True) + jnp.exp(snk - m)))
            o = jnp.dot(p.astype(BF16), vw_ref[rs, hs], preferred_element_type=F32)
            _store_heads(ow_ref.at[rs, :], o, g, seq)


def _attn_ctx(sink, qg, kg, vg, qw, kw, vw, seq):
    t_tokens = qg.shape[0]
    rows = 2 * seq
    spec = lambda wd: pl.BlockSpec((rows, wd), lambda i: (i, 0))
    return pl.pallas_call(
        functools.partial(_attn_ctx_kernel, seq=seq),
        grid=(t_tokens // rows,),
        in_specs=[pl.BlockSpec(memory_space=pltpu.SMEM),
                  spec(GLOB_Q), spec(GLOB_KVW), spec(GLOB_KVW), spec(WIN_Q), spec(WIN_KVW), spec(WIN_KVW)],
        out_specs=[spec(GLOB_Q), spec(WIN_Q)],
        out_shape=[jax.ShapeDtypeStruct((t_tokens, GLOB_Q), BF16),
                   jax.ShapeDtypeStruct((t_tokens, WIN_Q), BF16)],
        compiler_params=_params("arbitrary"),
        name="attn_ctx",
    )(sink, qg, kg, vg, qw, kw, vw)


def _attn_lat_kernel(sink_ref, qg_ref, kg_ref, vg_ref, cgk_ref, cgv_ref,
                     qw_ref, kw_ref, vw_ref, cwk_ref, cwv_ref, og_ref, ow_ref, *, seq, qb):
    bi = pl.program_id(1)
    span = qb + 2 * WINDOW
    start = pl.multiple_of(jnp.clip(bi * qb - WINDOW, 0, seq - span), HEAD_DIM)
    row = lax.broadcasted_iota(jnp.int32, (GROUP * qb, span), 0)
    col = lax.broadcasted_iota(jnp.int32, (GROUP * qb, span), 1)
    qpos = bi * qb + jnp.bitwise_and(row, qb - 1)
    kpos = start + col
    valid = jnp.abs(qpos - kpos) <= WINDOW

    for g in range(GLOB_KV):
        hs = slice(g * HEAD_DIM, (g + 1) * HEAD_DIM)
        q = _stack_heads(qg_ref[...], g)
        s1 = _scores(q, cgk_ref[:, hs].astype(BF16))
        s2 = _scores(q, kg_ref[:, hs])
        m = jnp.maximum(jnp.max(s1, axis=-1, keepdims=True), jnp.max(s2, axis=-1, keepdims=True))
        e1 = jnp.exp(s1 - m)
        e2 = jnp.exp(s2 - m)
        r = 1.0 / (jnp.sum(e1, axis=-1, keepdims=True) + jnp.sum(e2, axis=-1, keepdims=True))
        o = (jnp.dot((e1 * r).astype(BF16), cgv_ref[:, hs].astype(BF16), preferred_element_type=F32)
             + jnp.dot((e2 * r).astype(BF16), vg_ref[:, hs], preferred_element_type=F32))
        _store_heads(og_ref, o, g, qb)
        q = _stack_heads(qw_ref[...], g)
        s1 = _scores(q, cwk_ref[:, hs].astype(BF16))
        s2 = jnp.where(valid, _scores(q, kw_ref[pl.ds(start, span), hs]), NEG_INF)
        snk = _sink_column(sink_ref, g, qb)
        m = jnp.maximum(jnp.maximum(jnp.max(s1, axis=-1, keepdims=True),
                                    jnp.max(s2, axis=-1, keepdims=True)), snk)
        e1 = jnp.exp(s1 - m)
        e2 = jnp.exp(s2 - m)
        r = 1.0 / (jnp.sum(e1, axis=-1, keepdims=True) + jnp.sum(e2, axis=-1, keepdims=True)
                   + jnp.exp(snk - m))
        o = (jnp.dot((e1 * r).astype(BF16), cwv_ref[:, hs].astype(BF16), preferred_element_type=F32)
             + jnp.dot((e2 * r).astype(BF16), vw_ref[pl.ds(start, span), hs], preferred_element_type=F32))
        _store_heads(ow_ref, o, g, qb)


def _attn_lat(sink, layer, qg, kg, vg, cgk, cgv, qw, kw, vw, cwk, cwv, seq):
    t_tokens = qg.shape[0]
    nbatch = t_tokens // seq
    qb = 128
    nqb = seq // qb
    past = cgk.shape[2]
    qspec = lambda wd: pl.BlockSpec((qb, wd), lambda b, i: (b * nqb + i, 0))
    kvspec = lambda wd: pl.BlockSpec((seq, wd), lambda b, i: (b, 0))
    cspec = lambda wd: pl.BlockSpec((None, None, past, wd), lambda b, i: (b, layer, 0, 0))
    return pl.pallas_call(
        functools.partial(_attn_lat_kernel, seq=seq, qb=qb),
        grid=(nbatch, nqb),
        in_specs=[pl.BlockSpec(memory_space=pltpu.SMEM),
                  qspec(GLOB_Q), kvspec(GLOB_KVW), kvspec(GLOB_KVW), cspec(GLOB_KVW), cspec(GLOB_KVW),
                  qspec(WIN_Q), kvspec(WIN_KVW), kvspec(WIN_KVW), cspec(WIN_KVW), cspec(WIN_KVW)],
        out_specs=[qspec(GLOB_Q), qspec(WIN_Q)],
        out_shape=[jax.ShapeDtypeStruct((t_tokens, GLOB_Q), BF16),
                   jax.ShapeDtypeStruct((t_tokens, WIN_Q), BF16)],
        compiler_params=_params("arbitrary", "arbitrary"),
        name="attn_lat",
    )(sink, qg, kg, vg, cgk, cgv, qw, kw, vw, cwk, cwv)


def _outproj_kernel(x_ref, mod_ref, yc_ref, yg_ref, yw_ref, w_ref, o_ref):
    c1 = CONV_CH
    c2 = CONV_CH + GLOB_Q
    mix = (jnp.dot(yc_ref[...], w_ref[0:c1, :], preferred_element_type=F32)
           + jnp.dot(yg_ref[...], w_ref[c1:c2, :], preferred_element_type=F32)
           + jnp.dot(yw_ref[...], w_ref[c2:, :], preferred_element_type=F32))
    o_ref[...] = x_ref[...] + mod_ref[2:3, :] * mix


def _outproj(x, mod, row_of_tile, yc, yg, yw, w, layer):
    t_tokens, d = x.shape
    tm = 512
    row = lambda wd: pl.BlockSpec((tm, wd), lambda i: (i, 0))
    return pl.pallas_call(
        _outproj_kernel,
        grid=(t_tokens // tm,),
        in_specs=[row(d),
                  pl.BlockSpec((None, 6, d), lambda i: (row_of_tile(i, tm), 0, 0)),
                  row(CONV_CH), row(GLOB_Q), row(WIN_Q),
                  pl.BlockSpec((None,) + w.shape[1:], lambda i: (layer, 0, 0))],
        out_specs=row(d),
        out_shape=jax.ShapeDtypeStruct((t_tokens, d), F32),
        compiler_params=_params("arbitrary"),
        name="outproj",
    )(x, mod, yc, yg, yw, w)


def _mlp_kernel(*refs, final_norm):
    if final_norm:
        x_ref, mod_ref, g_ref, w1_ref, w2_ref, gf_ref, o_ref, h_ref, acc_ref = refs
    else:
        x_ref, mod_ref, g_ref, w1_ref, w2_ref, o_ref, h_ref, acc_ref = refs
    j = pl.program_id(1)

    @pl.when(j == 0)
    def _():
        h = _rms(x_ref[...], g_ref[...]) * (1.0 + mod_ref[4:5, :]) + mod_ref[3:4, :]
        h_ref[...] = h.astype(BF16)

    a = jnp.maximum(jnp.dot(h_ref[...], w1_ref[...], preferred_element_type=F32), 0.0)
    part = jnp.dot((a * a).astype(BF16), w2_ref[...], preferred_element_type=F32)

    @pl.when(j == 0)
    def _():
        acc_ref[...] = part

    @pl.when(j > 0)
    def _():
        acc_ref[...] += part

    @pl.when(j == pl.num_programs(1) - 1)
    def _():
        y = x_ref[...] + mod_ref[5:6, :] * acc_ref[...]
        if final_norm:
            y = _rms(y, gf_ref[...])
        o_ref[...] = y


def _mlp(x, mod, row_of_tile, g, w1, w2, layer, g_final):
    t_tokens, d = x.shape
    d_ff = w1.shape[2]
    tm, tf = 512, 512
    final_norm = g_final is not None
    in_specs = [
        pl.BlockSpec((tm, d), lambda i, j: (i, 0)),
        pl.BlockSpec((None, 6, d), lambda i, j: (row_of_tile(i, tm), 0, 0)),
        pl.BlockSpec((1, d), lambda i, j: (0, 0)),
        pl.BlockSpec((None, d, tf), lambda i, j: (layer, 0, j)),
        pl.BlockSpec((None, tf, d), lambda i, j: (layer, j, 0)),
    ]
    args = [x, mod, g, w1, w2]
    if final_norm:
        in_specs.append(pl.BlockSpec((1, d), lambda i, j: (0, 0)))
        args.append(g_final)
    return pl.pallas_call(
        functools.partial(_mlp_kernel, final_norm=final_norm),
        grid=(t_tokens // tm, d_ff // tf),
        in_specs=in_specs,
        out_specs=pl.BlockSpec((tm, d), lambda i, j: (i, 0)),
        out_shape=jax.ShapeDtypeStruct((t_tokens, d), F32),
        scratch_shapes=[pltpu.VMEM((tm, d), BF16), pltpu.VMEM((tm, d), F32)],
        compiler_params=_params("arbitrary", "arbitrary"),
        name="mlp_final" if final_norm else "mlp",
    )(*args)


def _rope_tables(n_tokens):
    rows = n_tokens // GRID_W
    r, col = jnp.meshgrid(jnp.arange(rows), jnp.arange(GRID_W), indexing='ij')
    r = r.reshape(-1).astype(F32)
    col = col.reshape(-1).astype(F32)
    n_freq = HEAD_DIM // 4
    inv = ROPE_THETA ** (-jnp.arange(n_freq, dtype=F32) / n_freq)
    ang_r = r[:, None] * inv
    ang_c = col[:, None] * inv
    ang = jnp.concatenate([ang_r, ang_r, ang_c, ang_c], axis=-1)
    sign = jnp.asarray(np.tile(np.repeat(np.array([-1.0, 1.0], np.float32), n_freq), 2))
    return jnp.cos(ang), jnp.sin(ang) * sign


def kernel(x_prompt, x_sample, cache_glob_k, cache_glob_v, cache_win_k, cache_win_v, c, c_ctx, w_ada, b_ada,
           g_attn, g_mlp, w_in, conv_w, conv_b, conv_ln_g, conv_ln_b, q_norm_g, k_norm_g, sink, w_out,
           w_mlp1, w_mlp2, g_final):
    batch, seq, d = x_prompt.shape
    dec_batch, dec_seq, _ = x_sample.shape
    depth = w_ada.shape[0]
    past = cache_glob_k.shape[2]
    ctx_row = dec_batch
    assert dec_batch < MOD_ROWS

    cin = jnp.concatenate([c, c_ctx[None, :], jnp.zeros((MOD_ROWS - dec_batch - 1, d), F32)], axis=0)
    mods = _ada(cin, w_ada, b_ada).reshape(depth, MOD_ROWS, 6, d)
    rope_tabs = _rope_tables(dec_seq)

    w_in_b = w_in.astype(BF16)
    w_out_b = w_out.astype(BF16)
    w1_b = w_mlp1.astype(BF16)
    w2_b = w_mlp2.astype(BF16)
    caches = [a.reshape(dec_batch, depth, past, -1) for a in (cache_glob_k, cache_glob_v, cache_win_k, cache_win_v)]

    ctx_rows = lambda i, tm: ctx_row
    lat_rows = lambda i, tm: (i * tm) // dec_seq

    xp = x_prompt.reshape(batch * seq, d)
    xs = x_sample.reshape(dec_batch * dec_seq, d)
    new_cache = [[], [], [], []]
    for l in range(depth):
        mod = mods[l]
        row = lambda a: a[l].reshape(1, -1)
        g_fin = g_final.reshape(1, d) if l == depth - 1 else None
        conv_args = (conv_w[l], row(conv_b), row(conv_ln_g), row(conv_ln_b))

        u, qg, kg, vg, qw, kw, vw, kgc, vgc, kwc, vwc = _inproj(
            xp, mod, ctx_rows, row(g_attn), w_in_b, l, row(q_norm_g), row(k_norm_g), None, seq, True)
        for acc, new in zip(new_cache, (kgc, vgc, kwc, vwc)):
            acc.append(new.reshape(batch, seq, -1, HEAD_DIM))
        yc = _conv(u, *conv_args, seq)
        yg, yw = _attn_ctx(sink[l], qg, kg, vg, qw, kw, vw, seq)
        xp = _outproj(xp, mod, ctx_rows, yc, yg, yw, w_out_b, l)
        xp = _mlp(xp, mod, ctx_rows, row(g_mlp), w1_b, w2_b, l, g_fin)

        u, qg, kg, vg, qw, kw, vw = _inproj(
            xs, mod, lat_rows, row(g_attn), w_in_b, l, row(q_norm_g), row(k_norm_g), rope_tabs, dec_seq, False)
        yc = _conv(u, *conv_args, dec_seq)
        yg, yw = _attn_lat(sink[l], l, qg, kg, vg, caches[0], caches[1], qw, kw, vw, caches[2], caches[3],
                           dec_seq)
        xs = _outproj(xs, mod, lat_rows, yc, yg, yw, w_out_b, l)
        xs = _mlp(xs, mod, lat_rows, row(g_mlp), w1_b, w2_b, l, g_fin)

    outs = [jnp.stack(a, axis=1) for a in new_cache]
    return (xp.reshape(batch, seq, d), xs.reshape(dec_batch, dec_seq, d), *outs)
```

```python
import functools

import jax
import jax.numpy as jnp
import numpy as np
from jax import lax
from jax.experimental import pallas as pl
from jax.experimental.pallas import tpu as pltpu

HEAD_DIM = 128
CONV_CH = 512
CONV_WIDTH = 31
GLOB_HEADS = 6
GLOB_KV = 2
WIN_HEADS = 6
WIN_KV = 2
WINDOW = 128
GRID_W = 64
ROPE_THETA = 10000.0
NORM_EPS = 1e-6
NEG_INF = -1e30

GLOB_Q = GLOB_HEADS * HEAD_DIM
GLOB_KVW = GLOB_KV * HEAD_DIM
WIN_Q = WIN_HEADS * HEAD_DIM
WIN_KVW = WIN_KV * HEAD_DIM
GROUP = GLOB_HEADS // GLOB_KV
C_U = 0
C_QG = 2 * CONV_CH
C_KG = C_QG + GLOB_Q
C_VG = C_KG + GLOB_KVW
C_QW = C_VG + GLOB_KVW
C_KW = C_QW + WIN_Q
C_VW = C_KW + WIN_KVW
IN_WIDTH = C_VW + WIN_KVW

LOG2E = float(np.log2(np.e))
Q_PRESCALE = HEAD_DIM ** -0.5 * LOG2E

MOD_ROWS = 16
CONV_HALO = 16
SUBLANES = 8
CONV_BLOCK = 256
V7X_VMEM_LIMIT = 56 * 1024 * 1024

BF16 = jnp.bfloat16
F32 = jnp.float32


def _params(*sem):
    return pltpu.CompilerParams(dimension_semantics=sem, vmem_limit_bytes=V7X_VMEM_LIMIT)


def _sigmoid(x):
    return 1.0 / (1.0 + jnp.exp(-x))


def _rms(x, gain):
    return x * lax.rsqrt(jnp.mean(x * x, axis=-1, keepdims=True) + NORM_EPS) * gain


def _ada_kernel(c_ref, w_ref, b_ref, o_ref):
    c = c_ref[...]
    a = (c * _sigmoid(c)).astype(BF16)
    o_ref[...] = jnp.dot(a, w_ref[...].astype(BF16), preferred_element_type=F32) + b_ref[...]


def _ada(cin, w_ada, b_ada):
    depth, d, n = w_ada.shape
    tn = 1024
    return pl.pallas_call(
        _ada_kernel,
        grid=(depth, n // tn),
        in_specs=[
            pl.BlockSpec((MOD_ROWS, d), lambda l, j: (0, 0)),
            pl.BlockSpec((None, d, tn), lambda l, j: (l, 0, j)),
            pl.BlockSpec((None, 1, tn), lambda l, j: (l, 0, j)),
        ],
        out_specs=pl.BlockSpec((None, MOD_ROWS, tn), lambda l, j: (l, 0, j)),
        out_shape=jax.ShapeDtypeStruct((depth, MOD_ROWS, n), F32),
        compiler_params=_params("arbitrary", "arbitrary"),
        name="ada_mod",
    )(cin, w_ada, b_ada.reshape(depth, 1, n))


def _inproj_kernel(*refs, rope, emit_cache, n_aliased):
    x_ref, mod_ref, g_ref, w_ref, qn_ref, kn_ref = refs[:6]
    pos = 6
    if rope:
        cos_ref, sin_ref = refs[pos:pos + 2]
        pos += 2
    pos += n_aliased
    u_ref, qg_ref, kg_ref, vg_ref, qw_ref, kw_ref, vw_ref = refs[pos:pos + 7]
    pos += 7
    if emit_cache:
        kgc_ref, vgc_ref, kwc_ref, vwc_ref = refs[pos:pos + 4]

    x = x_ref[...]
    h = _rms(x, g_ref[...]) * (1.0 + mod_ref[1:2, :]) + mod_ref[0:1, :]
    hb = h.astype(BF16)

    def proj(c0, c1):
        return jnp.dot(hb, w_ref[:, c0:c1], preferred_element_type=F32)

    if rope:
        cos = cos_ref[...]
        sin = sin_ref[...]
        lane = lax.broadcasted_iota(jnp.int32, cos.shape, 1)
        take_next = ((lane // (HEAD_DIM // 4)) % 2) == 0

        def rot(t):
            r = jnp.where(take_next, pltpu.roll(t, HEAD_DIM - HEAD_DIM // 4, 1),
                          pltpu.roll(t, HEAD_DIM // 4, 1))
            return t * cos + r * sin
    else:
        def rot(t):
            return t

    def heads(p, n):
        return [p[:, i * HEAD_DIM:(i + 1) * HEAD_DIM] for i in range(n)]

    u_ref[...] = proj(C_U, C_QG)

    qn = qn_ref[...]
    kn = kn_ref[...]
    for i, t in enumerate(heads(proj(C_QG, C_KG), GLOB_HEADS)):
        qg_ref[:, i * HEAD_DIM:(i + 1) * HEAD_DIM] = (rot(_rms(t, qn)) * Q_PRESCALE).astype(BF16)
    for i, t in enumerate(heads(proj(C_KG, C_VG), GLOB_KV)):
        t = _rms(t, kn)
        if emit_cache:
            kgc_ref[:, i * HEAD_DIM:(i + 1) * HEAD_DIM] = t
        kg_ref[:, i * HEAD_DIM:(i + 1) * HEAD_DIM] = rot(t).astype(BF16)
    p = proj(C_VG, C_QW)
    if emit_cache:
        vgc_ref[...] = p
    vg_ref[...] = p.astype(BF16)
    for i, t in enumerate(heads(proj(C_QW, C_KW), WIN_HEADS)):
        qw_ref[:, i * HEAD_DIM:(i + 1) * HEAD_DIM] = (rot(t) * Q_PRESCALE).astype(BF16)
    p = proj(C_KW, C_VW)
    if emit_cache:
        kwc_ref[...] = p
    for i, t in enumerate(heads(p, WIN_KV)):
        kw_ref[:, i * HEAD_DIM:(i + 1) * HEAD_DIM] = rot(t).astype(BF16)
    p = proj(C_VW, IN_WIDTH)
    if emit_cache:
        vwc_ref[...] = p
    vw_ref[...] = p.astype(BF16)


def _inproj(x, mod, row_of_tile, g, w, layer, qn, kn, rope_tabs, seq, prev_cache):
    t_tokens, d = x.shape
    tm = 256
    rope = rope_tabs is not None
    emit_cache = prev_cache is not None
    const = lambda i: (0, 0)
    in_specs = [
        pl.BlockSpec((tm, d), lambda i: (i, 0)),
        pl.BlockSpec((None, 6, d), lambda i: (row_of_tile(i, tm), 0, 0)),
        pl.BlockSpec((1, d), const),
        pl.BlockSpec((None, d, IN_WIDTH), lambda i: (layer, 0, 0)),
        pl.BlockSpec((1, HEAD_DIM), const),
        pl.BlockSpec((1, HEAD_DIM), const),
    ]
    args = [x, mod, g, w, qn, kn]
    if rope:
        per_seq = seq // tm
        in_specs += [pl.BlockSpec((tm, HEAD_DIM), lambda i: (i % per_seq, 0))] * 2
        args += list(rope_tabs)
    widths = [(C_QG - C_U, F32), (GLOB_Q, BF16), (GLOB_KVW, BF16), (GLOB_KVW, BF16),
              (WIN_Q, BF16), (WIN_KVW, BF16), (WIN_KVW, BF16)]
    out_specs = [pl.BlockSpec((tm, wd), lambda i: (i, 0)) for wd, _ in widths]
    out_shape = [jax.ShapeDtypeStruct((t_tokens, wd), dt) for wd, dt in widths]
    aliases = {}
    n_aliased = 0
    if emit_cache:
        per_seq = seq // tm
        depth = w.shape[0]
        n_aliased = len(prev_cache)
        for k, prev in enumerate(prev_cache):
            aliases[len(args)] = len(out_shape) + k
            in_specs.append(pl.BlockSpec(memory_space=pl.ANY))
            args.append(prev)
        for wd in (GLOB_KVW, GLOB_KVW, WIN_KVW, WIN_KVW):
            out_specs.append(pl.BlockSpec((None, None, tm, wd), lambda i: (i // per_seq, layer, i % per_seq, 0)))
            out_shape.append(jax.ShapeDtypeStruct((t_tokens // seq, depth, seq, wd), F32))
    return pl.pallas_call(
        functools.partial(_inproj_kernel, rope=rope, emit_cache=emit_cache, n_aliased=n_aliased),
        grid=(t_tokens // tm,),
        in_specs=in_specs,
        out_specs=out_specs,
        out_shape=out_shape,
        input_output_aliases=aliases,
        compiler_params=_params("arbitrary"),
        name="inproj_rope" if rope else "inproj_ctx",
    )(*args)


def _conv_kernel(*refs, blocks_per_seq):
    halo = blocks_per_seq > 1
    if halo:
        u_ref, up_ref, un_ref, w_ref, b_ref, lg_ref, lb_ref, o_ref, hp_ref, y_ref = refs
    else:
        u_ref, w_ref, b_ref, lg_ref, lb_ref, o_ref, hp_ref, y_ref = refs
    rows = u_ref.shape[0]

    def glu(u):
        return u[:, :CONV_CH] * _sigmoid(u[:, CONV_CH:])

    zeros = jnp.zeros((CONV_HALO, CONV_CH), F32)
    if halo:
        j = pl.program_id(0) % blocks_per_seq
        hp_ref[0:CONV_HALO, :] = jnp.where(j > 0, glu(up_ref[...]), zeros)
        hp_ref[CONV_HALO + rows:, :] = jnp.where(j < blocks_per_seq - 1, glu(un_ref[...]), zeros)
    else:
        hp_ref[0:CONV_HALO, :] = zeros
        hp_ref[CONV_HALO + rows:, :] = zeros
    hp_ref[CONV_HALO:CONV_HALO + rows, :] = glu(u_ref[...])

    rchunk = 128
    base = CONV_HALO - CONV_WIDTH // 2
    groups = -(-(base + CONV_WIDTH) // SUBLANES)
    for c in range(CONV_CH // HEAD_DIM):
        cs = slice(c * HEAD_DIM, (c + 1) * HEAD_DIM)
        for r in range(rows // rchunk):
            acc = None
            for s in range(SUBLANES):
                part = None
                for q in range(groups):
                    k = q * SUBLANES + s - base
                    if 0 <= k < CONV_WIDTH:
                        r0 = r * rchunk + q * SUBLANES
                        term = hp_ref[r0:r0 + rchunk + SUBLANES, cs] * w_ref[k:k + 1, cs]
                        part = term if part is None else part + term
                part = part[s:s + rchunk]
                acc = part if acc is None else acc + part
            y_ref[r * rchunk:(r + 1) * rchunk, cs] = acc

    y = y_ref[...] + b_ref[...]
    mu = jnp.mean(y, axis=-1, keepdims=True)
    yc = y - mu
    var = jnp.mean(yc * yc, axis=-1, keepdims=True)
    z = yc * lax.rsqrt(var + NORM_EPS) * lg_ref[...] + lb_ref[...]
    o_ref[...] = (z * _sigmoid(z)).astype(BF16)


def _conv(u, w, b, lg, lb, seq):
    t_tokens = u.shape[0]
    rows = CONV_BLOCK
    blocks_per_seq = seq // rows
    halo_per_block = rows // CONV_HALO
    n_halo = t_tokens // CONV_HALO
    const = lambda i: (0, 0)
    in_specs = [pl.BlockSpec((rows, 2 * CONV_CH), lambda i: (i, 0))]
    args = [u]
    if blocks_per_seq > 1:
        in_specs += [
            pl.BlockSpec((CONV_HALO, 2 * CONV_CH), lambda i: (jnp.maximum(i * halo_per_block - 1, 0), 0)),
            pl.BlockSpec((CONV_HALO, 2 * CONV_CH),
                         lambda i: (jnp.minimum((i + 1) * halo_per_block, n_halo - 1), 0)),
        ]
        args += [u, u]
    in_specs += [pl.BlockSpec((CONV_WIDTH, CONV_CH), const)] + [pl.BlockSpec((1, CONV_CH), const)] * 3
    args += [w, b, lg, lb]
    return pl.pallas_call(
        functools.partial(_conv_kernel, blocks_per_seq=blocks_per_seq),
        grid=(t_tokens // rows,),
        in_specs=in_specs,
        out_specs=pl.BlockSpec((rows, CONV_CH), lambda i: (i, 0)),
        out_shape=jax.ShapeDtypeStruct((t_tokens, CONV_CH), BF16),
        scratch_shapes=[pltpu.VMEM((rows + 2 * CONV_HALO, CONV_CH), F32),
                        pltpu.VMEM((rows, CONV_CH), F32)],
        compiler_params=_params("arbitrary"),
        name="conv_mixer",
    )(*args)


def _stack_heads(q, g):
    return jnp.concatenate(
        [q[:, (g * GROUP + j) * HEAD_DIM:(g * GROUP + j + 1) * HEAD_DIM] for j in range(GROUP)], axis=0)


def _scores(q, k):
    return lax.dot_general(q, k, (((1,), (1,)), ((), ())), preferred_element_type=F32)


def _sink_column(sink_ref, g, rows):
    return jnp.concatenate(
        [jnp.full((rows, 1), sink_ref[g * GROUP + j] * LOG2E, F32) for j in range(GROUP)], axis=0)


def _softmax_pv(pieces, snk):
    m = functools.reduce(jnp.maximum, [jnp.max(s, axis=-1, keepdims=True) for s, _ in pieces])
    if snk is not None:
        m = jnp.maximum(m, snk)
    es = [jnp.exp2(s - m) for s, _ in pieces]
    den = functools.reduce(jnp.add, [jnp.sum(e, axis=-1, keepdims=True) for e in es])
    if snk is not None:
        den = den + jnp.exp2(snk - m)
    o = functools.reduce(jnp.add, [jnp.dot(e.astype(BF16), v, preferred_element_type=F32)
                                   for e, (_, v) in zip(es, pieces)])
    return o * (1.0 / den)


def _store_heads(o_ref, o, g, rows):
    for j in range(GROUP):
        c0 = (g * GROUP + j) * HEAD_DIM
        o_ref[:, c0:c0 + HEAD_DIM] = o[j * rows:(j + 1) * rows].astype(o_ref.dtype)


def _attn_ctx_kernel(sink_ref, qg_ref, kg_ref, vg_ref, qw_ref, kw_ref, vw_ref, og_ref, ow_ref, *, seq):
    nb = qg_ref.shape[0] // seq
    for b in range(nb):
        rs = slice(b * seq, (b + 1) * seq)
        for g in range(GLOB_KV):
            hs = slice(g * HEAD_DIM, (g + 1) * HEAD_DIM)
            s = _scores(_stack_heads(qg_ref[rs, :], g), kg_ref[rs, hs])
            _store_heads(og_ref.at[rs, :], _softmax_pv([(s, vg_ref[rs, hs])], None), g, seq)
            s = _scores(_stack_heads(qw_ref[rs, :], g), kw_ref[rs, hs])
            o = _softmax_pv([(s, vw_ref[rs, hs])], _sink_column(sink_ref, g, seq))
            _store_heads(ow_ref.at[rs, :], o, g, seq)


def _attn_ctx(sink, qg, kg, vg, qw, kw, vw, seq):
    t_tokens = qg.shape[0]
    rows = 2 * seq
    spec = lambda wd: pl.BlockSpec((rows, wd), lambda i: (i, 0))
    return pl.pallas_call(
        functools.partial(_attn_ctx_kernel, seq=seq),
        grid=(t_tokens // rows,),
        in_specs=[pl.BlockSpec(memory_space=pltpu.SMEM),
                  spec(GLOB_Q), spec(GLOB_KVW), spec(GLOB_KVW), spec(WIN_Q), spec(WIN_KVW), spec(WIN_KVW)],
        out_specs=[spec(GLOB_Q), spec(WIN_Q)],
        out_shape=[jax.ShapeDtypeStruct((t_tokens, GLOB_Q), BF16),
                   jax.ShapeDtypeStruct((t_tokens, WIN_Q), BF16)],
        compiler_params=_params("arbitrary"),
        name="attn_ctx",
    )(sink, qg, kg, vg, qw, kw, vw)


def _attn_lat_kernel(sink_ref, qg_ref, kg_ref, vg_ref, cgk_ref, cgv_ref,
                     qw_ref, kw_ref, vw_ref, cwk_ref, cwv_ref, og_ref, ow_ref,
                     cgk_b, cgv_b, cwk_b, cwv_b, *, seq, qb):
    bi = pl.program_id(1)

    @pl.when(bi == 0)
    def _():
        for src, dst in ((cgk_ref, cgk_b), (cgv_ref, cgv_b), (cwk_ref, cwk_b), (cwv_ref, cwv_b)):
            dst[...] = src[...].astype(BF16)

    span = qb + 2 * WINDOW
    start = pl.multiple_of(jnp.clip(bi * qb - WINDOW, 0, seq - span), HEAD_DIM)
    qpos = bi * qb + lax.broadcasted_iota(jnp.int32, (qb, span), 0)
    kpos = start + lax.broadcasted_iota(jnp.int32, (qb, span), 1)
    band = jnp.where(jnp.abs(qpos - kpos) <= WINDOW, 0.0, NEG_INF)
    band = jnp.concatenate([band] * GROUP, axis=0)

    for g in range(GLOB_KV):
        hs = slice(g * HEAD_DIM, (g + 1) * HEAD_DIM)
        q = _stack_heads(qg_ref[...], g)
        o = _softmax_pv([(_scores(q, cgk_b[:, hs]), cgv_b[:, hs]),
                         (_scores(q, kg_ref[:, hs]), vg_ref[:, hs])], None)
        _store_heads(og_ref, o, g, qb)
        q = _stack_heads(qw_ref[...], g)
        s_loc = _scores(q, kw_ref[pl.ds(start, span), hs]) + band
        o = _softmax_pv([(_scores(q, cwk_b[:, hs]), cwv_b[:, hs]),
                         (s_loc, vw_ref[pl.ds(start, span), hs])], _sink_column(sink_ref, g, qb))
        _store_heads(ow_ref, o, g, qb)


def _attn_lat(sink, layer, qg, kg, vg, cgk, cgv, qw, kw, vw, cwk, cwv, seq):
    t_tokens = qg.shape[0]
    nbatch = t_tokens // seq
    qb = 128
    nqb = seq // qb
    past = cgk.shape[2]
    qspec = lambda wd: pl.BlockSpec((qb, wd), lambda b, i: (b * nqb + i, 0))
    kvspec = lambda wd: pl.BlockSpec((seq, wd), lambda b, i: (b, 0))
    cspec = lambda wd: pl.BlockSpec((None, None, past, wd), lambda b, i: (b, layer, 0, 0))
    return pl.pallas_call(
        functools.partial(_attn_lat_kernel, seq=seq, qb=qb),
        grid=(nbatch, nqb),
        in_specs=[pl.BlockSpec(memory_space=pltpu.SMEM),
                  qspec(GLOB_Q), kvspec(GLOB_KVW), kvspec(GLOB_KVW), cspec(GLOB_KVW), cspec(GLOB_KVW),
                  qspec(WIN_Q), kvspec(WIN_KVW), kvspec(WIN_KVW), cspec(WIN_KVW), cspec(WIN_KVW)],
        out_specs=[qspec(GLOB_Q), qspec(WIN_Q)],
        out_shape=[jax.ShapeDtypeStruct((t_tokens, GLOB_Q), BF16),
                   jax.ShapeDtypeStruct((t_tokens, WIN_Q), BF16)],
        scratch_shapes=[pltpu.VMEM((past, wd), BF16) for wd in (GLOB_KVW, GLOB_KVW, WIN_KVW, WIN_KVW)],
        compiler_params=_params("arbitrary", "arbitrary"),
        name="attn_lat",
    )(sink, qg, kg, vg, cgk, cgv, qw, kw, vw, cwk, cwv)


def _outproj_kernel(x_ref, mod_ref, yc_ref, yg_ref, yw_ref, w_ref, o_ref):
    c1 = CONV_CH
    c2 = CONV_CH + GLOB_Q
    mix = (jnp.dot(yc_ref[...], w_ref[0:c1, :], preferred_element_type=F32)
           + jnp.dot(yg_ref[...], w_ref[c1:c2, :], preferred_element_type=F32)
           + jnp.dot(yw_ref[...], w_ref[c2:, :], preferred_element_type=F32))
    o_ref[...] = x_ref[...] + mod_ref[2:3, :] * mix


def _outproj(x, mod, row_of_tile, yc, yg, yw, w, layer):
    t_tokens, d = x.shape
    tm = 512
    row = lambda wd: pl.BlockSpec((tm, wd), lambda i: (i, 0))
    return pl.pallas_call(
        _outproj_kernel,
        grid=(t_tokens // tm,),
        in_specs=[row(d),
                  pl.BlockSpec((None, 6, d), lambda i: (row_of_tile(i, tm), 0, 0)),
                  row(CONV_CH), row(GLOB_Q), row(WIN_Q),
                  pl.BlockSpec((None,) + w.shape[1:], lambda i: (layer, 0, 0))],
        out_specs=row(d),
        out_shape=jax.ShapeDtypeStruct((t_tokens, d), F32),
        compiler_params=_params("arbitrary"),
        name="outproj",
    )(x, mod, yc, yg, yw, w)


def _mlp_kernel(*refs, final_norm):
    if final_norm:
        x_ref, mod_ref, g_ref, w1_ref, w2_ref, gf_ref, o_ref, h_ref, acc_ref = refs
    else:
        x_ref, mod_ref, g_ref, w1_ref, w2_ref, o_ref, h_ref, acc_ref = refs
    j = pl.program_id(1)

    @pl.when(j == 0)
    def _():
        h = _rms(x_ref[...], g_ref[...]) * (1.0 + mod_ref[4:5, :]) + mod_ref[3:4, :]
        h_ref[...] = h.astype(BF16)
        acc_ref[...] = jnp.zeros_like(acc_ref)

    a = jnp.maximum(jnp.dot(h_ref[...], w1_ref[...], preferred_element_type=F32), 0.0)
    acc_ref[...] += jnp.dot((a * a).astype(BF16), w2_ref[...], preferred_element_type=F32)

    @pl.when(j == pl.num_programs(1) - 1)
    def _():
        y = x_ref[...] + mod_ref[5:6, :] * acc_ref[...]
        if final_norm:
            y = _rms(y, gf_ref[...])
        o_ref[...] = y


def _mlp(x, mod, row_of_tile, g, w1, w2, layer, g_final):
    t_tokens, d = x.shape
    d_ff = w1.shape[2]
    tm, tf = 512, 1024
    final_norm = g_final is not None
    in_specs = [
        pl.BlockSpec((tm, d), lambda i, j: (i, 0)),
        pl.BlockSpec((None, 6, d), lambda i, j: (row_of_tile(i, tm), 0, 0)),
        pl.BlockSpec((1, d), lambda i, j: (0, 0)),
        pl.BlockSpec((None, d, tf), lambda i, j: (layer, 0, j)),
        pl.BlockSpec((None, tf, d), lambda i, j: (layer, j, 0)),
    ]
    args = [x, mod, g, w1, w2]
    if final_norm:
        in_specs.append(pl.BlockSpec((1, d), lambda i, j: (0, 0)))
        args.append(g_final)
    return pl.pallas_call(
        functools.partial(_mlp_kernel, final_norm=final_norm),
        grid=(t_tokens // tm, d_ff // tf),
        in_specs=in_specs,
        out_specs=pl.BlockSpec((tm, d), lambda i, j: (i, 0)),
        out_shape=jax.ShapeDtypeStruct((t_tokens, d), F32),
        scratch_shapes=[pltpu.VMEM((tm, d), BF16), pltpu.VMEM((tm, d), F32)],
        compiler_params=_params("arbitrary", "arbitrary"),
        name="mlp_final" if final_norm else "mlp",
    )(*args)


def _rope_tables(n_tokens):
    rows = n_tokens // GRID_W
    r, col = jnp.meshgrid(jnp.arange(rows), jnp.arange(GRID_W), indexing='ij')
    r = r.reshape(-1).astype(F32)
    col = col.reshape(-1).astype(F32)
    n_freq = HEAD_DIM // 4
    inv = ROPE_THETA ** (-jnp.arange(n_freq, dtype=F32) / n_freq)
    ang_r = r[:, None] * inv
    ang_c = col[:, None] * inv
    ang = jnp.concatenate([ang_r, ang_r, ang_c, ang_c], axis=-1)
    sign = jnp.asarray(np.tile(np.repeat(np.array([-1.0, 1.0], np.float32), n_freq), 2))
    return jnp.cos(ang), jnp.sin(ang) * sign


def kernel(x_prompt, x_sample, cache_glob_k, cache_glob_v, cache_win_k, cache_win_v, c, c_ctx, w_ada, b_ada,
           g_attn, g_mlp, w_in, conv_w, conv_b, conv_ln_g, conv_ln_b, q_norm_g, k_norm_g, sink, w_out,
           w_mlp1, w_mlp2, g_final):
    batch, seq, d = x_prompt.shape
    dec_batch, dec_seq, _ = x_sample.shape
    depth = w_ada.shape[0]
    past = cache_glob_k.shape[2]
    ctx_row = dec_batch
    assert dec_batch < MOD_ROWS

    cin = jnp.concatenate([c, c_ctx[None, :], jnp.zeros((MOD_ROWS - dec_batch - 1, d), F32)], axis=0)
    mods = _ada(cin, w_ada, b_ada).reshape(depth, MOD_ROWS, 6, d)
    rope_tabs = _rope_tables(dec_seq)

    w_in_b = w_in.astype(BF16)
    w_out_b = w_out.astype(BF16)
    w1_b = w_mlp1.astype(BF16)
    w2_b = w_mlp2.astype(BF16)
    caches = [a.reshape(dec_batch, depth, past, -1) for a in (cache_glob_k, cache_glob_v, cache_win_k, cache_win_v)]

    ctx_rows = lambda i, tm: ctx_row
    lat_rows = lambda i, tm: (i * tm) // dec_seq

    xp = x_prompt.reshape(batch * seq, d)
    xs = x_sample.reshape(dec_batch * dec_seq, d)
    new_cache = []
    for l in range(depth):
        mod = mods[l]
        row = lambda a: a[l].reshape(1, -1)
        g_fin = g_final.reshape(1, d) if l == depth - 1 else None
        conv_args = (conv_w[l], row(conv_b), row(conv_ln_g), row(conv_ln_b))

        u, qg, kg, vg, qw, kw, vw, *new_cache = _inproj(
            xp, mod, ctx_rows, row(g_attn), w_in_b, l, row(q_norm_g), row(k_norm_g), None, seq, new_cache)
        yc = _conv(u, *conv_args, seq)
        yg, yw = _attn_ctx(sink[l], qg, kg, vg, qw, kw, vw, seq)
        xp = _outproj(xp, mod, ctx_rows, yc, yg, yw, w_out_b, l)
        xp = _mlp(xp, mod, ctx_rows, row(g_mlp), w1_b, w2_b, l, g_fin)

        u, qg, kg, vg, qw, kw, vw = _inproj(
            xs, mod, lat_rows, row(g_attn), w_in_b, l, row(q_norm_g), row(k_norm_g), rope_tabs, dec_seq, None)
        yc = _conv(u, *conv_args, dec_seq)
        yg, yw = _attn_lat(sink[l], l, qg, kg, vg, caches[0], caches[1], qw, kw, vw, caches[2], caches[3],
                           dec_seq)
        xs = _outproj(xs, mod, lat_rows, yc, yg, yw, w_out_b, l)
        xs = _mlp(xs, mod, lat_rows, row(g_mlp), w1_b, w2_b, l, g_fin)

    outs = [a.reshape(batch, depth, seq, -1, HEAD_DIM) for a in new_cache]
    return (xp.reshape(batch, seq, d), xs.reshape(dec_batch, dec_seq, d), *outs)
```

```python
import functools

import jax
import jax.numpy as jnp
import numpy as np
from jax import lax
from jax.experimental import pallas as pl
from jax.experimental.pallas import tpu as pltpu

HEAD_DIM = 128
CONV_CH = 512
CONV_WIDTH = 31
GLOB_HEADS = 6
GLOB_KV = 2
WIN_HEADS = 6
WIN_KV = 2
WINDOW = 128
GRID_W = 64
ROPE_THETA = 10000.0
NORM_EPS = 1e-6
NEG_INF = -1e30

GLOB_Q = GLOB_HEADS * HEAD_DIM
GLOB_KVW = GLOB_KV * HEAD_DIM
WIN_Q = WIN_HEADS * HEAD_DIM
WIN_KVW = WIN_KV * HEAD_DIM
GROUP = GLOB_HEADS // GLOB_KV
C_U = 0
C_QG = 2 * CONV_CH
C_KG = C_QG + GLOB_Q
C_VG = C_KG + GLOB_KVW
C_QW = C_VG + GLOB_KVW
C_KW = C_QW + WIN_Q
C_VW = C_KW + WIN_KVW
IN_WIDTH = C_VW + WIN_KVW

LOG2E = float(np.log2(np.e))
Q_PRESCALE = HEAD_DIM ** -0.5 * LOG2E

MOD_ROWS = 16
CONV_HALO = 16
SUBLANES = 8
CONV_BLOCK = 256
V7X_VMEM_LIMIT = 56 * 1024 * 1024

BF16 = jnp.bfloat16
F32 = jnp.float32


def _params(*sem):
    return pltpu.CompilerParams(dimension_semantics=sem, vmem_limit_bytes=V7X_VMEM_LIMIT)


def _sigmoid(x):
    return 1.0 / (1.0 + jnp.exp(-x))


def _rms(x, gain):
    return x * lax.rsqrt(jnp.mean(x * x, axis=-1, keepdims=True) + NORM_EPS) * gain


def _ada_kernel(c_ref, w_ref, b_ref, o_ref):
    c = c_ref[...]
    a = (c * _sigmoid(c)).astype(BF16)
    o_ref[...] = jnp.dot(a, w_ref[...].astype(BF16), preferred_element_type=F32) + b_ref[...]


def _ada(cin, w_ada, b_ada):
    depth, d, n = w_ada.shape
    tn = 1024
    return pl.pallas_call(
        _ada_kernel,
        grid=(depth, n // tn),
        in_specs=[
            pl.BlockSpec((MOD_ROWS, d), lambda l, j: (0, 0)),
            pl.BlockSpec((None, d, tn), lambda l, j: (l, 0, j)),
            pl.BlockSpec((None, 1, tn), lambda l, j: (l, 0, j)),
        ],
        out_specs=pl.BlockSpec((None, MOD_ROWS, tn), lambda l, j: (l, 0, j)),
        out_shape=jax.ShapeDtypeStruct((depth, MOD_ROWS, n), F32),
        compiler_params=_params("arbitrary", "arbitrary"),
        name="ada_mod",
    )(cin, w_ada, b_ada.reshape(depth, 1, n))


def _store_cache(ref, cols, t):
    nb, seq, _ = ref.shape
    for b in range(nb):
        ref[b, :, cols] = t[b * seq:(b + 1) * seq]


def _project(x_ref, mod_ref, g_ref, w_ref, qn_ref, kn_ref, rope_refs, outs, caches):
    u_ref, qg_ref, kg_ref, vg_ref, qw_ref, kw_ref, vw_ref = outs
    h = _rms(x_ref[...], g_ref[...]) * (1.0 + mod_ref[1:2, :]) + mod_ref[0:1, :]
    hb = h.astype(BF16)

    def proj(c0, c1):
        return jnp.dot(hb, w_ref[:, c0:c1], preferred_element_type=F32)

    if rope_refs is not None:
        cos = rope_refs[0][...]
        sin = rope_refs[1][...]
        lane = lax.broadcasted_iota(jnp.int32, cos.shape, 1)
        take_next = ((lane // (HEAD_DIM // 4)) % 2) == 0

        def rot(t):
            r = jnp.where(take_next, pltpu.roll(t, HEAD_DIM - HEAD_DIM // 4, 1),
                          pltpu.roll(t, HEAD_DIM // 4, 1))
            return t * cos + r * sin
    else:
        def rot(t):
            return t

    def heads(p, n):
        return [(slice(i * HEAD_DIM, (i + 1) * HEAD_DIM), p[:, i * HEAD_DIM:(i + 1) * HEAD_DIM])
                for i in range(n)]

    u_ref[...] = proj(C_U, C_QG)

    qn = qn_ref[...]
    kn = kn_ref[...]
    for cols, t in heads(proj(C_QG, C_KG), GLOB_HEADS):
        qg_ref[:, cols] = (rot(_rms(t, qn)) * Q_PRESCALE).astype(BF16)
    for cols, t in heads(proj(C_KG, C_VG), GLOB_KV):
        t = _rms(t, kn)
        if caches is not None:
            _store_cache(caches[0], cols, t)
        kg_ref[:, cols] = rot(t).astype(BF16)
    p = proj(C_VG, C_QW)
    if caches is not None:
        _store_cache(caches[1], slice(None), p)
    vg_ref[...] = p.astype(BF16)
    for cols, t in heads(proj(C_QW, C_KW), WIN_HEADS):
        qw_ref[:, cols] = (rot(t) * Q_PRESCALE).astype(BF16)
    p = proj(C_KW, C_VW)
    if caches is not None:
        _store_cache(caches[2], slice(None), p)
    for cols, t in heads(p, WIN_KV):
        kw_ref[:, cols] = rot(t).astype(BF16)
    p = proj(C_VW, IN_WIDTH)
    if caches is not None:
        _store_cache(caches[3], slice(None), p)
    vw_ref[...] = p.astype(BF16)


PROJ_WIDTHS = ((C_QG - C_U, F32), (GLOB_Q, BF16), (GLOB_KVW, BF16), (GLOB_KVW, BF16),
               (WIN_Q, BF16), (WIN_KVW, BF16), (WIN_KVW, BF16))


def _inproj_kernel(*refs, rope, n_aliased):
    x_ref, mod_ref, g_ref, w_ref, qn_ref, kn_ref = refs[:6]
    pos = 6
    rope_refs = None
    if rope:
        rope_refs = refs[pos:pos + 2]
        pos += 2
    pos += n_aliased
    outs = refs[pos:pos + len(PROJ_WIDTHS)]
    caches = refs[pos + len(PROJ_WIDTHS):] or None
    _project(x_ref, mod_ref, g_ref, w_ref, qn_ref, kn_ref, rope_refs, outs, caches)


def _inproj(x, mod, row_of_tile, g, w, layer, qn, kn, rope_tabs, seq, prev_cache):
    t_tokens, d = x.shape
    tm = 512
    rope = rope_tabs is not None
    const = lambda i: (0, 0)
    in_specs = [
        pl.BlockSpec((tm, d), lambda i: (i, 0)),
        pl.BlockSpec((None, 6, d), lambda i: (row_of_tile(i, tm), 0, 0)),
        pl.BlockSpec((1, d), const),
        pl.BlockSpec((None, d, IN_WIDTH), lambda i: (layer, 0, 0), pipeline_mode=pl.Buffered(1)),
        pl.BlockSpec((1, HEAD_DIM), const),
        pl.BlockSpec((1, HEAD_DIM), const),
    ]
    args = [x, mod, g, w, qn, kn]
    if rope:
        per_seq = seq // tm
        in_specs += [pl.BlockSpec((tm, HEAD_DIM), lambda i: (i % per_seq, 0))] * 2
        args += list(rope_tabs)
    out_specs = [pl.BlockSpec((tm, wd), lambda i: (i, 0)) for wd, _ in PROJ_WIDTHS]
    out_shape = [jax.ShapeDtypeStruct((t_tokens, wd), dt) for wd, dt in PROJ_WIDTHS]
    aliases = {}
    n_aliased = 0
    if prev_cache is not None:
        nb = tm // seq
        depth = w.shape[0]
        n_aliased = len(prev_cache)
        for k, prev in enumerate(prev_cache):
            aliases[len(args)] = len(out_shape) + k
            in_specs.append(pl.BlockSpec(memory_space=pl.ANY))
            args.append(prev)
        for wd in (GLOB_KVW, GLOB_KVW, WIN_KVW, WIN_KVW):
            out_specs.append(pl.BlockSpec((nb, None, seq, wd), lambda i: (i, layer, 0, 0)))
            out_shape.append(jax.ShapeDtypeStruct((t_tokens // seq, depth, seq, wd), F32))
    return pl.pallas_call(
        functools.partial(_inproj_kernel, rope=rope, n_aliased=n_aliased),
        grid=(t_tokens // tm,),
        in_specs=in_specs,
        out_specs=out_specs,
        out_shape=out_shape,
        input_output_aliases=aliases,
        compiler_params=_params("arbitrary"),
        name="inproj_rope" if rope else "inproj_ctx",
    )(*args)


def _conv_body(u_ref, halo, w_ref, b_ref, lg_ref, lb_ref, o_ref, hp_ref, y_ref):
    rows = u_ref.shape[0]

    def glu(u):
        return u[:, :CONV_CH] * _sigmoid(u[:, CONV_CH:])

    zeros = jnp.zeros((CONV_HALO, CONV_CH), F32)
    if halo is not None:
        up_ref, un_ref, has_prev, has_next = halo
        hp_ref[0:CONV_HALO, :] = jnp.where(has_prev, glu(up_ref[...]), zeros)
        hp_ref[CONV_HALO + rows:, :] = jnp.where(has_next, glu(un_ref[...]), zeros)
    else:
        hp_ref[0:CONV_HALO, :] = zeros
        hp_ref[CONV_HALO + rows:, :] = zeros
    hp_ref[CONV_HALO:CONV_HALO + rows, :] = glu(u_ref[...])

    rchunk = 128
    base = CONV_HALO - CONV_WIDTH // 2
    groups = -(-(base + CONV_WIDTH) // SUBLANES)
    for c in range(CONV_CH // HEAD_DIM):
        cs = slice(c * HEAD_DIM, (c + 1) * HEAD_DIM)
        for r in range(rows // rchunk):
            acc = None
            for s in range(SUBLANES):
                part = None
                for q in range(groups):
                    k = q * SUBLANES + s - base
                    if 0 <= k < CONV_WIDTH:
                        r0 = r * rchunk + q * SUBLANES
                        term = hp_ref[r0:r0 + rchunk + SUBLANES, cs] * w_ref[k:k + 1, cs]
                        part = term if part is None else part + term
                part = part[s:s + rchunk]
                acc = part if acc is None else acc + part
            y_ref[r * rchunk:(r + 1) * rchunk, cs] = acc

    y = y_ref[...] + b_ref[...]
    mu = jnp.mean(y, axis=-1, keepdims=True)
    yc = y - mu
    var = jnp.mean(yc * yc, axis=-1, keepdims=True)
    z = yc * lax.rsqrt(var + NORM_EPS) * lg_ref[...] + lb_ref[...]
    o_ref[...] = (z * _sigmoid(z)).astype(BF16)


def _conv_kernel(*refs, blocks_per_seq):
    if blocks_per_seq > 1:
        u_ref, up_ref, un_ref = refs[:3]
        j = pl.program_id(0) % blocks_per_seq
        halo = (up_ref, un_ref, j > 0, j < blocks_per_seq - 1)
        rest = refs[3:]
    else:
        u_ref, halo, rest = refs[0], None, refs[1:]
    _conv_body(u_ref, halo, *rest)


def _conv(u, w, b, lg, lb, seq):
    t_tokens = u.shape[0]
    rows = CONV_BLOCK
    blocks_per_seq = seq // rows
    halo_per_block = rows // CONV_HALO
    n_halo = t_tokens // CONV_HALO
    const = lambda i: (0, 0)
    in_specs = [pl.BlockSpec((rows, 2 * CONV_CH), lambda i: (i, 0))]
    args = [u]
    if blocks_per_seq > 1:
        in_specs += [
            pl.BlockSpec((CONV_HALO, 2 * CONV_CH), lambda i: (jnp.maximum(i * halo_per_block - 1, 0), 0)),
            pl.BlockSpec((CONV_HALO, 2 * CONV_CH),
                         lambda i: (jnp.minimum((i + 1) * halo_per_block, n_halo - 1), 0)),
        ]
        args += [u, u]
    in_specs += [pl.BlockSpec((CONV_WIDTH, CONV_CH), const)] + [pl.BlockSpec((1, CONV_CH), const)] * 3
    args += [w, b, lg, lb]
    return pl.pallas_call(
        functools.partial(_conv_kernel, blocks_per_seq=blocks_per_seq),
        grid=(t_tokens // rows,),
        in_specs=in_specs,
        out_specs=pl.BlockSpec((rows, CONV_CH), lambda i: (i, 0)),
        out_shape=jax.ShapeDtypeStruct((t_tokens, CONV_CH), BF16),
        scratch_shapes=[pltpu.VMEM((rows + 2 * CONV_HALO, CONV_CH), F32),
                        pltpu.VMEM((rows, CONV_CH), F32)],
        compiler_params=_params("arbitrary"),
        name="conv_mixer",
    )(*args)


def _stack_heads(q, g):
    return jnp.concatenate(
        [q[:, (g * GROUP + j) * HEAD_DIM:(g * GROUP + j + 1) * HEAD_DIM] for j in range(GROUP)], axis=0)


def _scores(q, k):
    return lax.dot_general(q, k, (((1,), (1,)), ((), ())), preferred_element_type=F32)


def _sink_column(sink_ref, g, rows):
    return jnp.concatenate(
        [jnp.full((rows, 1), sink_ref[g * GROUP + j] * LOG2E, F32) for j in range(GROUP)], axis=0)


def _store_heads(o_ref, o, g, rows):
    for j in range(GROUP):
        c0 = (g * GROUP + j) * HEAD_DIM
        o_ref[:, c0:c0 + HEAD_DIM] = o[j * rows:(j + 1) * rows].astype(o_ref.dtype)


def _attention_jobs(jobs):
    n = len(jobs)
    pieces = [None] * n
    soft = [None] * n
    for t in range(n + 2):
        if t < n:
            pieces[t] = jobs[t][0]()
        if 0 <= t - 2 < n:
            es, rden = soft[t - 2]
            o = functools.reduce(jnp.add, [jnp.dot(e, v, preferred_element_type=F32)
                                           for e, (_, v) in zip(es, pieces[t - 2])])
            jobs[t - 2][2](o * rden)
        if 0 <= t - 1 < n:
            snk = jobs[t - 1][1]
            m = functools.reduce(jnp.maximum, [jnp.max(s, axis=-1, keepdims=True) for s, _ in pieces[t - 1]])
            if snk is not None:
                m = jnp.maximum(m, snk)
            es = [jnp.exp2(s - m) for s, _ in pieces[t - 1]]
            den = functools.reduce(jnp.add, [jnp.sum(e, axis=-1, keepdims=True) for e in es])
            if snk is not None:
                den = den + jnp.exp2(snk - m)
            soft[t - 1] = ([e.astype(BF16) for e in es], 1.0 / den)


def _attn_ctx_kernel(sink_ref, qg_ref, kg_ref, vg_ref, qw_ref, kw_ref, vw_ref, og_ref, ow_ref, *, seq):
    jobs = []
    for b in range(qg_ref.shape[0] // seq):
        rs = slice(b * seq, (b + 1) * seq)
        for g in range(GLOB_KV):
            hs = slice(g * HEAD_DIM, (g + 1) * HEAD_DIM)

            def glob_scores(rs=rs, hs=hs, g=g):
                return [(_scores(_stack_heads(qg_ref[rs, :], g), kg_ref[rs, hs]), vg_ref[rs, hs])]

            def win_scores(rs=rs, hs=hs, g=g):
                return [(_scores(_stack_heads(qw_ref[rs, :], g), kw_ref[rs, hs]), vw_ref[rs, hs])]

            jobs.append((glob_scores, None, functools.partial(_store_heads, og_ref.at[rs, :], g=g, rows=seq)))
            jobs.append((win_scores, _sink_column(sink_ref, g, seq),
                         functools.partial(_store_heads, ow_ref.at[rs, :], g=g, rows=seq)))
    _attention_jobs(jobs)


def _attn_ctx(sink, qg, kg, vg, qw, kw, vw, seq):
    t_tokens = qg.shape[0]
    rows = 2 * seq
    spec = lambda wd: pl.BlockSpec((rows, wd), lambda i: (i, 0))
    return pl.pallas_call(
        functools.partial(_attn_ctx_kernel, seq=seq),
        grid=(t_tokens // rows,),
        in_specs=[pl.BlockSpec(memory_space=pltpu.SMEM),
                  spec(GLOB_Q), spec(GLOB_KVW), spec(GLOB_KVW), spec(WIN_Q), spec(WIN_KVW), spec(WIN_KVW)],
        out_specs=[spec(GLOB_Q), spec(WIN_Q)],
        out_shape=[jax.ShapeDtypeStruct((t_tokens, GLOB_Q), BF16),
                   jax.ShapeDtypeStruct((t_tokens, WIN_Q), BF16)],
        compiler_params=_params("arbitrary"),
        name="attn_ctx",
    )(sink, qg, kg, vg, qw, kw, vw)


def _attn_lat_kernel(sink_ref, qg_ref, kg_ref, vg_ref, cgk_ref, cgv_ref,
                     qw_ref, kw_ref, vw_ref, cwk_ref, cwv_ref, og_ref, ow_ref,
                     cgk_b, cgv_b, cwk_b, cwv_b, *, seq, qb):
    bi = pl.program_id(1)

    @pl.when(bi == 0)
    def _():
        for src, dst in ((cgk_ref, cgk_b), (cgv_ref, cgv_b), (cwk_ref, cwk_b), (cwv_ref, cwv_b)):
            dst[...] = src[...].astype(BF16)

    span = qb + 2 * WINDOW
    start = pl.multiple_of(jnp.clip(bi * qb - WINDOW, 0, seq - span), HEAD_DIM)
    qpos = bi * qb + lax.broadcasted_iota(jnp.int32, (qb, span), 0)
    kpos = start + lax.broadcasted_iota(jnp.int32, (qb, span), 1)
    band = jnp.where(jnp.abs(qpos - kpos) <= WINDOW, 0.0, NEG_INF)
    band = jnp.concatenate([band] * GROUP, axis=0)

    jobs = []
    for g in range(GLOB_KV):
        hs = slice(g * HEAD_DIM, (g + 1) * HEAD_DIM)

        def glob_scores(hs=hs, g=g):
            q = _stack_heads(qg_ref[...], g)
            return [(_scores(q, cgk_b[:, hs]), cgv_b[:, hs]), (_scores(q, kg_ref[:, hs]), vg_ref[:, hs])]

        def win_scores(hs=hs, g=g):
            q = _stack_heads(qw_ref[...], g)
            s_loc = _scores(q, kw_ref[pl.ds(start, span), hs]) + band
            return [(_scores(q, cwk_b[:, hs]), cwv_b[:, hs]), (s_loc, vw_ref[pl.ds(start, span), hs])]

        jobs.append((glob_scores, None, functools.partial(_store_heads, og_ref, g=g, rows=qb)))
        jobs.append((win_scores, _sink_column(sink_ref, g, qb),
                     functools.partial(_store_heads, ow_ref, g=g, rows=qb)))
    _attention_jobs(jobs)


def _attn_lat(sink, layer, qg, kg, vg, cgk, cgv, qw, kw, vw, cwk, cwv, seq):
    t_tokens = qg.shape[0]
    nbatch = t_tokens // seq
    qb = 128
    nqb = seq // qb
    past = cgk.shape[2]
    qspec = lambda wd: pl.BlockSpec((qb, wd), lambda b, i: (b * nqb + i, 0))
    kvspec = lambda wd: pl.BlockSpec((seq, wd), lambda b, i: (b, 0))
    cspec = lambda wd: pl.BlockSpec((None, None, past, wd), lambda b, i: (b, layer, 0, 0))
    return pl.pallas_call(
        functools.partial(_attn_lat_kernel, seq=seq, qb=qb),
        grid=(nbatch, nqb),
        in_specs=[pl.BlockSpec(memory_space=pltpu.SMEM),
                  qspec(GLOB_Q), kvspec(GLOB_KVW), kvspec(GLOB_KVW), cspec(GLOB_KVW), cspec(GLOB_KVW),
                  qspec(WIN_Q), kvspec(WIN_KVW), kvspec(WIN_KVW), cspec(WIN_KVW), cspec(WIN_KVW)],
        out_specs=[qspec(GLOB_Q), qspec(WIN_Q)],
        out_shape=[jax.ShapeDtypeStruct((t_tokens, GLOB_Q), BF16),
                   jax.ShapeDtypeStruct((t_tokens, WIN_Q), BF16)],
        scratch_shapes=[pltpu.VMEM((past, wd), BF16) for wd in (GLOB_KVW, GLOB_KVW, WIN_KVW, WIN_KVW)],
        compiler_params=_params("arbitrary", "arbitrary"),
        name="attn_lat",
    )(sink, qg, kg, vg, cgk, cgv, qw, kw, vw, cwk, cwv)


def _outproj_kernel(x_ref, mod_ref, g_ref, yc_ref, yg_ref, yw_ref, w_ref, o_ref, h_ref, *, rchunk):
    c1 = CONV_CH
    c2 = CONV_CH + GLOB_Q
    for r in range(x_ref.shape[0] // rchunk):
        rs = slice(r * rchunk, (r + 1) * rchunk)
        mix = (jnp.dot(yc_ref[rs, :], w_ref[0:c1, :], preferred_element_type=F32)
               + jnp.dot(yg_ref[rs, :], w_ref[c1:c2, :], preferred_element_type=F32)
               + jnp.dot(yw_ref[rs, :], w_ref[c2:, :], preferred_element_type=F32))
        x1 = x_ref[rs, :] + mod_ref[2:3, :] * mix
        o_ref[rs, :] = x1
        h = _rms(x1, g_ref[...]) * (1.0 + mod_ref[4:5, :]) + mod_ref[3:4, :]
        h_ref[rs, :] = h.astype(BF16)


def _outproj(x, mod, row_of_tile, g_mlp, yc, yg, yw, w, layer):
    t_tokens, d = x.shape
    tm = 512
    row = lambda wd: pl.BlockSpec((tm, wd), lambda i: (i, 0))
    return pl.pallas_call(
        functools.partial(_outproj_kernel, rchunk=256),
        grid=(t_tokens // tm,),
        in_specs=[row(d),
                  pl.BlockSpec((None, 6, d), lambda i: (row_of_tile(i, tm), 0, 0)),
                  pl.BlockSpec((1, d), lambda i: (0, 0)),
                  row(CONV_CH), row(GLOB_Q), row(WIN_Q),
                  pl.BlockSpec((None,) + w.shape[1:], lambda i: (layer, 0, 0), pipeline_mode=pl.Buffered(1))],
        out_specs=[row(d), row(d)],
        out_shape=[jax.ShapeDtypeStruct((t_tokens, d), F32), jax.ShapeDtypeStruct((t_tokens, d), BF16)],
        compiler_params=_params("arbitrary"),
        name="outproj",
    )(x, mod, g_mlp, yc, yg, yw, w)


def _mlp_kernel(*refs, final_norm):
    if final_norm:
        x_ref, h_ref, mod_ref, w1_ref, w2_ref, gf_ref, o_ref, acc_ref = refs
    else:
        x_ref, h_ref, mod_ref, w1_ref, w2_ref, o_ref, acc_ref = refs
    j = pl.program_id(1)
    last = pl.num_programs(1) - 1

    def partial_sum():
        a = jnp.maximum(jnp.dot(h_ref[...], w1_ref[...], preferred_element_type=F32), 0.0)
        return jnp.dot((a * a).astype(BF16), w2_ref[...], preferred_element_type=F32)

    @pl.when(j == 0)
    def _():
        acc_ref[...] = partial_sum()

    @pl.when(jnp.logical_and(j > 0, j < last))
    def _():
        acc_ref[...] += partial_sum()

    @pl.when(j == last)
    def _():
        y = x_ref[...] + mod_ref[5:6, :] * (acc_ref[...] + partial_sum())
        if final_norm:
            y = _rms(y, gf_ref[...])
        o_ref[...] = y


def _mlp(x, h, mod, row_of_tile, w1, w2, layer, g_final):
    t_tokens, d = x.shape
    d_ff = w1.shape[2]
    tm, tf = 512, 1024
    assert d_ff // tf >= 2
    final_norm = g_final is not None
    in_specs = [
        pl.BlockSpec((tm, d), lambda i, j: (i, 0)),
        pl.BlockSpec((tm, d), lambda i, j: (i, 0)),
        pl.BlockSpec((None, 6, d), lambda i, j: (row_of_tile(i, tm), 0, 0)),
        pl.BlockSpec((None, d, tf), lambda i, j: (layer, 0, j)),
        pl.BlockSpec((None, tf, d), lambda i, j: (layer, j, 0)),
    ]
    args = [x, h, mod, w1, w2]
    if final_norm:
        in_specs.append(pl.BlockSpec((1, d), lambda i, j: (0, 0)))
        args.append(g_final)
    return pl.pallas_call(
        functools.partial(_mlp_kernel, final_norm=final_norm),
        grid=(t_tokens // tm, d_ff // tf),
        in_specs=in_specs,
        out_specs=pl.BlockSpec((tm, d), lambda i, j: (i, 0)),
        out_shape=jax.ShapeDtypeStruct((t_tokens, d), F32),
        scratch_shapes=[pltpu.VMEM((tm, d), F32)],
        compiler_params=_params("arbitrary", "arbitrary"),
        name="mlp_final" if final_norm else "mlp",
    )(*args)


def _rope_tables(n_tokens):
    rows = n_tokens // GRID_W
    r, col = jnp.meshgrid(jnp.arange(rows), jnp.arange(GRID_W), indexing='ij')
    r = r.reshape(-1).astype(F32)
    col = col.reshape(-1).astype(F32)
    n_freq = HEAD_DIM // 4
    inv = ROPE_THETA ** (-jnp.arange(n_freq, dtype=F32) / n_freq)
    ang_r = r[:, None] * inv
    ang_c = col[:, None] * inv
    ang = jnp.concatenate([ang_r, ang_r, ang_c, ang_c], axis=-1)
    sign = jnp.asarray(np.tile(np.repeat(np.array([-1.0, 1.0], np.float32), n_freq), 2))
    return jnp.cos(ang), jnp.sin(ang) * sign


def kernel(x_prompt, x_sample, cache_glob_k, cache_glob_v, cache_win_k, cache_win_v, c, c_ctx, w_ada, b_ada,
           g_attn, g_mlp, w_in, conv_w, conv_b, conv_ln_g, conv_ln_b, q_norm_g, k_norm_g, sink, w_out,
           w_mlp1, w_mlp2, g_final):
    batch, seq, d = x_prompt.shape
    dec_batch, dec_seq, _ = x_sample.shape
    depth = w_ada.shape[0]
    past = cache_glob_k.shape[2]
    ctx_row = dec_batch
    assert dec_batch < MOD_ROWS

    cin = jnp.concatenate([c, c_ctx[None, :], jnp.zeros((MOD_ROWS - dec_batch - 1, d), F32)], axis=0)
    mods = _ada(cin, w_ada, b_ada).reshape(depth, MOD_ROWS, 6, d)
    rope_tabs = _rope_tables(dec_seq)

    w_in_b = w_in.astype(BF16)
    w_out_b = w_out.astype(BF16)
    w1_b = w_mlp1.astype(BF16)
    w2_b = w_mlp2.astype(BF16)
    caches = [a.reshape(dec_batch, depth, past, -1) for a in (cache_glob_k, cache_glob_v, cache_win_k, cache_win_v)]

    ctx_rows = lambda i, tm: ctx_row
    lat_rows = lambda i, tm: (i * tm) // dec_seq

    xp = x_prompt.reshape(batch * seq, d)
    xs = x_sample.reshape(dec_batch * dec_seq, d)
    new_cache = []
    for l in range(depth):
        mod = mods[l]
        row = lambda a: a[l].reshape(1, -1)
        g_fin = g_final.reshape(1, d) if l == depth - 1 else None
        conv_args = (conv_w[l], row(conv_b), row(conv_ln_g), row(conv_ln_b))

        u, qg, kg, vg, qw, kw, vw, *new_cache = _inproj(
            xp, mod, ctx_rows, row(g_attn), w_in_b, l, row(q_norm_g), row(k_norm_g), None, seq, new_cache)
        yc = _conv(u, *conv_args, seq)
        yg, yw = _attn_ctx(sink[l], qg, kg, vg, qw, kw, vw, seq)
        xp, hp = _outproj(xp, mod, ctx_rows, row(g_mlp), yc, yg, yw, w_out_b, l)
        xp = _mlp(xp, hp, mod, ctx_rows, w1_b, w2_b, l, g_fin)

        u, qg, kg, vg, qw, kw, vw = _inproj(
            xs, mod, lat_rows, row(g_attn), w_in_b, l, row(q_norm_g), row(k_norm_g), rope_tabs, dec_seq, None)
        yc = _conv(u, *conv_args, dec_seq)
        yg, yw = _attn_lat(sink[l], l, qg, kg, vg, caches[0], caches[1], qw, kw, vw, caches[2], caches[3],
                           dec_seq)
        xs, hs = _outproj(xs, mod, lat_rows, row(g_mlp), yc, yg, yw, w_out_b, l)
        xs = _mlp(xs, hs, mod, lat_rows, w1_b, w2_b, l, g_fin)

    outs = [a.reshape(batch, depth, seq, -1, HEAD_DIM) for a in new_cache]
    return (xp.reshape(batch, seq, d), xs.reshape(dec_batch, dec_seq, d), *outs)
```

```python
import functools

import jax
import jax.numpy as jnp
import numpy as np
from jax import lax
from jax.experimental import pallas as pl
from jax.experimental.pallas import tpu as pltpu

HEAD_DIM = 128
CONV_CH = 512
CONV_WIDTH = 31
GLOB_HEADS = 6
GLOB_KV = 2
WIN_HEADS = 6
WIN_KV = 2
WINDOW = 128
GRID_W = 64
ROPE_THETA = 10000.0
NORM_EPS = 1e-6
NEG_INF = -1e30

GLOB_Q = GLOB_HEADS * HEAD_DIM
GLOB_KVW = GLOB_KV * HEAD_DIM
WIN_Q = WIN_HEADS * HEAD_DIM
WIN_KVW = WIN_KV * HEAD_DIM
GROUP = GLOB_HEADS // GLOB_KV
C_U = 0
C_QG = 2 * CONV_CH
C_KG = C_QG + GLOB_Q
C_VG = C_KG + GLOB_KVW
C_QW = C_VG + GLOB_KVW
C_KW = C_QW + WIN_Q
C_VW = C_KW + WIN_KVW
IN_WIDTH = C_VW + WIN_KVW

LOG2E = float(np.log2(np.e))
Q_PRESCALE = HEAD_DIM ** -0.5 * LOG2E

MOD_ROWS = 16
CONV_HALO = 16
SUBLANES = 8
CONV_BLOCK = 256
QSUB = 128
V7X_VMEM_LIMIT = 56 * 1024 * 1024

BF16 = jnp.bfloat16
F32 = jnp.float32


def _params(*sem):
    return pltpu.CompilerParams(dimension_semantics=sem, vmem_limit_bytes=V7X_VMEM_LIMIT)


def _sigmoid(x):
    return 1.0 / (1.0 + jnp.exp(-x))


def _rms(x, gain):
    return x * lax.rsqrt(jnp.mean(x * x, axis=-1, keepdims=True) + NORM_EPS) * gain


def _ada_kernel(c_ref, w_ref, b_ref, o_ref):
    c = c_ref[...]
    a = (c * _sigmoid(c)).astype(BF16)
    o_ref[...] = jnp.dot(a, w_ref[...].astype(BF16), preferred_element_type=F32) + b_ref[...]


def _ada(cin, w_ada, b_ada):
    depth, d, n = w_ada.shape
    tn = 1024
    return pl.pallas_call(
        _ada_kernel,
        grid=(depth, n // tn),
        in_specs=[
            pl.BlockSpec((MOD_ROWS, d), lambda l, j: (0, 0)),
            pl.BlockSpec((None, d, tn), lambda l, j: (l, 0, j)),
            pl.BlockSpec((None, 1, tn), lambda l, j: (l, 0, j)),
        ],
        out_specs=pl.BlockSpec((None, MOD_ROWS, tn), lambda l, j: (l, 0, j)),
        out_shape=jax.ShapeDtypeStruct((depth, MOD_ROWS, n), F32),
        compiler_params=_params("arbitrary", "arbitrary"),
        name="ada_mod",
    )(cin, w_ada, b_ada.reshape(depth, 1, n))


def _store_cache(ref, cols, t):
    nb, seq, kv, _ = ref.shape
    heads = range(kv)[slice(cols.start and cols.start // HEAD_DIM, cols.stop and cols.stop // HEAD_DIM)]
    for b in range(nb):
        for i, h in enumerate(heads):
            ref[b, :, h, :] = t[b * seq:(b + 1) * seq, i * HEAD_DIM:(i + 1) * HEAD_DIM]


def _project(x_ref, mod_ref, g_ref, w_ref, qn_ref, kn_ref, rope_refs, outs, caches):
    u_ref, qg_ref, kg_ref, vg_ref, qw_ref, kw_ref, vw_ref = outs
    h = _rms(x_ref[...], g_ref[...]) * (1.0 + mod_ref[1:2, :]) + mod_ref[0:1, :]
    hb = h.astype(BF16)

    def proj(c0, c1):
        return jnp.dot(hb, w_ref[:, c0:c1], preferred_element_type=F32)

    if rope_refs is not None:
        cos = rope_refs[0][...]
        sin = rope_refs[1][...]
        lane = lax.broadcasted_iota(jnp.int32, cos.shape, 1)
        take_next = ((lane // (HEAD_DIM // 4)) % 2) == 0

        def rot(t):
            r = jnp.where(take_next, pltpu.roll(t, HEAD_DIM - HEAD_DIM // 4, 1),
                          pltpu.roll(t, HEAD_DIM // 4, 1))
            return t * cos + r * sin
    else:
        def rot(t):
            return t

    def heads(p, n):
        return [(slice(i * HEAD_DIM, (i + 1) * HEAD_DIM), p[:, i * HEAD_DIM:(i + 1) * HEAD_DIM])
                for i in range(n)]

    u_ref[...] = proj(C_U, C_QG)

    qn = qn_ref[...]
    kn = kn_ref[...]
    for cols, t in heads(proj(C_QG, C_KG), GLOB_HEADS):
        qg_ref[:, cols] = (rot(_rms(t, qn)) * Q_PRESCALE).astype(BF16)
    for cols, t in heads(proj(C_KG, C_VG), GLOB_KV):
        t = _rms(t, kn)
        if caches is not None:
            _store_cache(caches[0], cols, t)
        kg_ref[:, cols] = rot(t).astype(BF16)
    p = proj(C_VG, C_QW)
    if caches is not None:
        _store_cache(caches[1], slice(None), p)
    vg_ref[...] = p.astype(BF16)
    for cols, t in heads(proj(C_QW, C_KW), WIN_HEADS):
        qw_ref[:, cols] = (rot(t) * Q_PRESCALE).astype(BF16)
    p = proj(C_KW, C_VW)
    if caches is not None:
        _store_cache(caches[2], slice(None), p)
    for cols, t in heads(p, WIN_KV):
        kw_ref[:, cols] = rot(t).astype(BF16)
    p = proj(C_VW, IN_WIDTH)
    if caches is not None:
        _store_cache(caches[3], slice(None), p)
    vw_ref[...] = p.astype(BF16)


PROJ_WIDTHS = ((C_QG - C_U, F32), (GLOB_Q, BF16), (GLOB_KVW, BF16), (GLOB_KVW, BF16),
               (WIN_Q, BF16), (WIN_KVW, BF16), (WIN_KVW, BF16))


def _inproj_kernel(*refs, rope, n_aliased):
    x_ref, mod_ref, g_ref, w_ref, qn_ref, kn_ref = refs[:6]
    pos = 6
    rope_refs = None
    if rope:
        rope_refs = refs[pos:pos + 2]
        pos += 2
    pos += n_aliased
    outs = refs[pos:pos + len(PROJ_WIDTHS)]
    caches = refs[pos + len(PROJ_WIDTHS):] or None
    _project(x_ref, mod_ref, g_ref, w_ref, qn_ref, kn_ref, rope_refs, outs, caches)


def _inproj(x, mod, row_of_tile, g, w, layer, qn, kn, rope_tabs, seq, prev_cache):
    t_tokens, d = x.shape
    tm = 512
    rope = rope_tabs is not None
    const = lambda i: (0, 0)
    in_specs = [
        pl.BlockSpec((tm, d), lambda i: (i, 0)),
        pl.BlockSpec((None, 6, d), lambda i: (row_of_tile(i, tm), 0, 0)),
        pl.BlockSpec((1, d), const),
        pl.BlockSpec((None, d, IN_WIDTH), lambda i: (layer, 0, 0), pipeline_mode=pl.Buffered(1)),
        pl.BlockSpec((1, HEAD_DIM), const),
        pl.BlockSpec((1, HEAD_DIM), const),
    ]
    args = [x, mod, g, w, qn, kn]
    if rope:
        per_seq = seq // tm
        in_specs += [pl.BlockSpec((tm, HEAD_DIM), lambda i: (i % per_seq, 0))] * 2
        args += list(rope_tabs)
    out_specs = [pl.BlockSpec((tm, wd), lambda i: (i, 0)) for wd, _ in PROJ_WIDTHS]
    out_shape = [jax.ShapeDtypeStruct((t_tokens, wd), dt) for wd, dt in PROJ_WIDTHS]
    aliases = {}
    n_aliased = 0
    if prev_cache is not None:
        nb = tm // seq
        depth = w.shape[0]
        n_aliased = len(prev_cache)
        for k, prev in enumerate(prev_cache):
            aliases[len(args)] = len(out_shape) + k
            in_specs.append(pl.BlockSpec(memory_space=pl.ANY))
            args.append(prev)
        for wd in (GLOB_KVW, GLOB_KVW, WIN_KVW, WIN_KVW):
            out_specs.append(pl.BlockSpec((nb, None, seq, wd // HEAD_DIM, HEAD_DIM),
                                          lambda i: (i, layer, 0, 0, 0)))
            out_shape.append(jax.ShapeDtypeStruct((t_tokens // seq, depth, seq, wd // HEAD_DIM, HEAD_DIM), F32))
    return pl.pallas_call(
        functools.partial(_inproj_kernel, rope=rope, n_aliased=n_aliased),
        grid=(t_tokens // tm,),
        in_specs=in_specs,
        out_specs=out_specs,
        out_shape=out_shape,
        input_output_aliases=aliases,
        compiler_params=_params("arbitrary"),
        name="inproj_rope" if rope else "inproj_ctx",
    )(*args)


def _conv_body(u_ref, halo, w_ref, b_ref, lg_ref, lb_ref, o_ref, hp_ref, y_ref):
    rows = u_ref.shape[0]

    def glu(u):
        return u[:, :CONV_CH] * _sigmoid(u[:, CONV_CH:])

    zeros = jnp.zeros((CONV_HALO, CONV_CH), F32)
    if halo is not None:
        up_ref, un_ref, has_prev, has_next = halo
        hp_ref[0:CONV_HALO, :] = jnp.where(has_prev, glu(up_ref[...]), zeros)
        hp_ref[CONV_HALO + rows:, :] = jnp.where(has_next, glu(un_ref[...]), zeros)
    else:
        hp_ref[0:CONV_HALO, :] = zeros
        hp_ref[CONV_HALO + rows:, :] = zeros
    hp_ref[CONV_HALO:CONV_HALO + rows, :] = glu(u_ref[...])

    rchunk = 128
    base = CONV_HALO - CONV_WIDTH // 2
    groups = -(-(base + CONV_WIDTH) // SUBLANES)
    for c in range(CONV_CH // HEAD_DIM):
        cs = slice(c * HEAD_DIM, (c + 1) * HEAD_DIM)
        for r in range(rows // rchunk):
            acc = None
            for s in range(SUBLANES):
                part = None
                for q in range(groups):
                    k = q * SUBLANES + s - base
                    if 0 <= k < CONV_WIDTH:
                        r0 = r * rchunk + q * SUBLANES
                        term = hp_ref[r0:r0 + rchunk + SUBLANES, cs] * w_ref[k:k + 1, cs]
                        part = term if part is None else part + term
                part = part[s:s + rchunk]
                acc = part if acc is None else acc + part
            y_ref[r * rchunk:(r + 1) * rchunk, cs] = acc

    y = y_ref[...] + b_ref[...]
    mu = jnp.mean(y, axis=-1, keepdims=True)
    yc = y - mu
    var = jnp.mean(yc * yc, axis=-1, keepdims=True)
    z = yc * lax.rsqrt(var + NORM_EPS) * lg_ref[...] + lb_ref[...]
    o_ref[...] = (z * _sigmoid(z)).astype(BF16)


def _conv_kernel(*refs, blocks_per_seq):
    if blocks_per_seq > 1:
        u_ref, up_ref, un_ref = refs[:3]
        j = pl.program_id(0) % blocks_per_seq
        halo = (up_ref, un_ref, j > 0, j < blocks_per_seq - 1)
        rest = refs[3:]
    else:
        u_ref, halo, rest = refs[0], None, refs[1:]
    _conv_body(u_ref, halo, *rest)


def _conv(u, w, b, lg, lb, seq):
    t_tokens = u.shape[0]
    rows = CONV_BLOCK
    blocks_per_seq = seq // rows
    halo_per_block = rows // CONV_HALO
    n_halo = t_tokens // CONV_HALO
    const = lambda i: (0, 0)
    in_specs = [pl.BlockSpec((rows, 2 * CONV_CH), lambda i: (i, 0))]
    args = [u]
    if blocks_per_seq > 1:
        in_specs += [
            pl.BlockSpec((CONV_HALO, 2 * CONV_CH), lambda i: (jnp.maximum(i * halo_per_block - 1, 0), 0)),
            pl.BlockSpec((CONV_HALO, 2 * CONV_CH),
                         lambda i: (jnp.minimum((i + 1) * halo_per_block, n_halo - 1), 0)),
        ]
        args += [u, u]
    in_specs += [pl.BlockSpec((CONV_WIDTH, CONV_CH), const)] + [pl.BlockSpec((1, CONV_CH), const)] * 3
    args += [w, b, lg, lb]
    return pl.pallas_call(
        functools.partial(_conv_kernel, blocks_per_seq=blocks_per_seq),
        grid=(t_tokens // rows,),
        in_specs=in_specs,
        out_specs=pl.BlockSpec((rows, CONV_CH), lambda i: (i, 0)),
        out_shape=jax.ShapeDtypeStruct((t_tokens, CONV_CH), BF16),
        scratch_shapes=[pltpu.VMEM((rows + 2 * CONV_HALO, CONV_CH), F32),
                        pltpu.VMEM((rows, CONV_CH), F32)],
        compiler_params=_params("arbitrary"),
        name="conv_mixer",
    )(*args)


def _stack_heads(q, g):
    return jnp.concatenate(
        [q[:, (g * GROUP + j) * HEAD_DIM:(g * GROUP + j + 1) * HEAD_DIM] for j in range(GROUP)], axis=0)


def _scores(q, k):
    return lax.dot_general(q, k, (((1,), (1,)), ((), ())), preferred_element_type=F32)


def _sink_column(sink_ref, g, rows):
    return jnp.concatenate(
        [jnp.full((rows, 1), sink_ref[g * GROUP + j] * LOG2E, F32) for j in range(GROUP)], axis=0)


def _store_heads(o_ref, o, g, rows):
    for j in range(GROUP):
        c0 = (g * GROUP + j) * HEAD_DIM
        o_ref[:, c0:c0 + HEAD_DIM] = o[j * rows:(j + 1) * rows].astype(o_ref.dtype)


def _attention_jobs(jobs):
    n = len(jobs)
    pieces = [None] * n
    soft = [None] * n
    for t in range(n + 2):
        if t < n:
            pieces[t] = jobs[t][0]()
        if 0 <= t - 2 < n:
            es, rden = soft[t - 2]
            o = functools.reduce(jnp.add, [jnp.dot(e, v, preferred_element_type=F32)
                                           for e, (_, v) in zip(es, pieces[t - 2])])
            jobs[t - 2][2](o * rden)
        if 0 <= t - 1 < n:
            snk = jobs[t - 1][1]
            m = functools.reduce(jnp.maximum, [jnp.max(s, axis=-1, keepdims=True) for s, _ in pieces[t - 1]])
            if snk is not None:
                m = jnp.maximum(m, snk)
            es = [jnp.exp2(s - m) for s, _ in pieces[t - 1]]
            den = functools.reduce(jnp.add, [jnp.sum(e, axis=-1, keepdims=True) for e in es])
            if snk is not None:
                den = den + jnp.exp2(snk - m)
            soft[t - 1] = ([e.astype(BF16) for e in es], 1.0 / den)


def _attn_ctx_kernel(sink_ref, qg_ref, kg_ref, vg_ref, qw_ref, kw_ref, vw_ref, og_ref, ow_ref, *, seq):
    jobs = []
    for b in range(qg_ref.shape[0] // seq):
        rs = slice(b * seq, (b + 1) * seq)
        for g in range(GLOB_KV):
            hs = slice(g * HEAD_DIM, (g + 1) * HEAD_DIM)

            def glob_scores(rs=rs, hs=hs, g=g):
                return [(_scores(_stack_heads(qg_ref[rs, :], g), kg_ref[rs, hs]), vg_ref[rs, hs])]

            def win_scores(rs=rs, hs=hs, g=g):
                return [(_scores(_stack_heads(qw_ref[rs, :], g), kw_ref[rs, hs]), vw_ref[rs, hs])]

            jobs.append((glob_scores, None, functools.partial(_store_heads, og_ref.at[rs, :], g=g, rows=seq)))
            jobs.append((win_scores, _sink_column(sink_ref, g, seq),
                         functools.partial(_store_heads, ow_ref.at[rs, :], g=g, rows=seq)))
    _attention_jobs(jobs)


def _attn_ctx(sink, qg, kg, vg, qw, kw, vw, seq):
    t_tokens = qg.shape[0]
    rows = 2 * seq
    spec = lambda wd: pl.BlockSpec((rows, wd), lambda i: (i, 0))
    return pl.pallas_call(
        functools.partial(_attn_ctx_kernel, seq=seq),
        grid=(t_tokens // rows,),
        in_specs=[pl.BlockSpec(memory_space=pltpu.SMEM),
                  spec(GLOB_Q), spec(GLOB_KVW), spec(GLOB_KVW), spec(WIN_Q), spec(WIN_KVW), spec(WIN_KVW)],
        out_specs=[spec(GLOB_Q), spec(WIN_Q)],
        out_shape=[jax.ShapeDtypeStruct((t_tokens, GLOB_Q), BF16),
                   jax.ShapeDtypeStruct((t_tokens, WIN_Q), BF16)],
        compiler_params=_params("arbitrary"),
        name="attn_ctx",
    )(sink, qg, kg, vg, qw, kw, vw)


def _attn_lat_kernel(sink_ref, qg_ref, kg_ref, vg_ref, cgk_b, cgv_b,
                     qw_ref, kw_ref, vw_ref, cwk_b, cwv_b, og_ref, ow_ref, *, seq, qb):
    bi = pl.program_id(1)
    span = QSUB + 2 * WINDOW
    jobs = []
    for sb in range(qb // QSUB):
        rs = slice(sb * QSUB, (sb + 1) * QSUB)
        q0 = bi * qb + sb * QSUB
        start = pl.multiple_of(jnp.clip(q0 - WINDOW, 0, seq - span), HEAD_DIM)
        qpos = q0 + lax.broadcasted_iota(jnp.int32, (QSUB, span), 0)
        kpos = start + lax.broadcasted_iota(jnp.int32, (QSUB, span), 1)
        band = jnp.where(jnp.abs(qpos - kpos) <= WINDOW, 0.0, NEG_INF)
        band = jnp.concatenate([band] * GROUP, axis=0)
        for g in range(GLOB_KV):
            hs = slice(g * HEAD_DIM, (g + 1) * HEAD_DIM)

            def glob_scores(rs=rs, hs=hs, g=g):
                q = _stack_heads(qg_ref[rs, :], g)
                return [(_scores(q, cgk_b[:, hs]), cgv_b[:, hs]), (_scores(q, kg_ref[:, hs]), vg_ref[:, hs])]

            def win_scores(rs=rs, hs=hs, g=g, start=start, band=band):
                q = _stack_heads(qw_ref[rs, :], g)
                s_loc = _scores(q, kw_ref[pl.ds(start, span), hs]) + band
                return [(_scores(q, cwk_b[:, hs]), cwv_b[:, hs]), (s_loc, vw_ref[pl.ds(start, span), hs])]

            jobs.append((glob_scores, None, functools.partial(_store_heads, og_ref.at[rs, :], g=g, rows=QSUB)))
            jobs.append((win_scores, _sink_column(sink_ref, g, QSUB),
                         functools.partial(_store_heads, ow_ref.at[rs, :], g=g, rows=QSUB)))
    _attention_jobs(jobs)


def _attn_lat(sink, layer, qg, kg, vg, cgk, cgv, qw, kw, vw, cwk, cwv, seq):
    t_tokens = qg.shape[0]
    nbatch = t_tokens // seq
    qb = QSUB
    nqb = seq // qb
    past = cgk.shape[2]
    qspec = lambda wd: pl.BlockSpec((qb, wd), lambda b, i: (b * nqb + i, 0))
    kvspec = lambda wd: pl.BlockSpec((seq, wd), lambda b, i: (b, 0))
    cspec = lambda wd: pl.BlockSpec((None, None, past, wd), lambda b, i: (b, layer, 0, 0))
    return pl.pallas_call(
        functools.partial(_attn_lat_kernel, seq=seq, qb=qb),
        grid=(nbatch, nqb),
        in_specs=[pl.BlockSpec(memory_space=pltpu.SMEM),
                  qspec(GLOB_Q), kvspec(GLOB_KVW), kvspec(GLOB_KVW), cspec(GLOB_KVW), cspec(GLOB_KVW),
                  qspec(WIN_Q), kvspec(WIN_KVW), kvspec(WIN_KVW), cspec(WIN_KVW), cspec(WIN_KVW)],
        out_specs=[qspec(GLOB_Q), qspec(WIN_Q)],
        out_shape=[jax.ShapeDtypeStruct((t_tokens, GLOB_Q), BF16),
                   jax.ShapeDtypeStruct((t_tokens, WIN_Q), BF16)],
        compiler_params=_params("arbitrary", "arbitrary"),
        name="attn_lat",
    )(sink, qg, kg, vg, cgk, cgv, qw, kw, vw, cwk, cwv)


def _outproj_kernel(x_ref, mod_ref, g_ref, yc_ref, yg_ref, yw_ref, w_ref, o_ref, h_ref, *, rchunk):
    c1 = CONV_CH
    c2 = CONV_CH + GLOB_Q
    for r in range(x_ref.shape[0] // rchunk):
        rs = slice(r * rchunk, (r + 1) * rchunk)
        mix = (jnp.dot(yc_ref[rs, :], w_ref[0:c1, :], preferred_element_type=F32)
               + jnp.dot(yg_ref[rs, :], w_ref[c1:c2, :], preferred_element_type=F32)
               + jnp.dot(yw_ref[rs, :], w_ref[c2:, :], preferred_element_type=F32))
        x1 = x_ref[rs, :] + mod_ref[2:3, :] * mix
        o_ref[rs, :] = x1
        h = _rms(x1, g_ref[...]) * (1.0 + mod_ref[4:5, :]) + mod_ref[3:4, :]
        h_ref[rs, :] = h.astype(BF16)


def _outproj(x, mod, row_of_tile, g_mlp, yc, yg, yw, w, layer):
    t_tokens, d = x.shape
    tm = 512
    row = lambda wd: pl.BlockSpec((tm, wd), lambda i: (i, 0))
    return pl.pallas_call(
        functools.partial(_outproj_kernel, rchunk=256),
        grid=(t_tokens // tm,),
        in_specs=[row(d),
                  pl.BlockSpec((None, 6, d), lambda i: (row_of_tile(i, tm), 0, 0)),
                  pl.BlockSpec((1, d), lambda i: (0, 0)),
                  row(CONV_CH), row(GLOB_Q), row(WIN_Q),
                  pl.BlockSpec((None,) + w.shape[1:], lambda i: (layer, 0, 0), pipeline_mode=pl.Buffered(1))],
        out_specs=[row(d), row(d)],
        out_shape=[jax.ShapeDtypeStruct((t_tokens, d), F32), jax.ShapeDtypeStruct((t_tokens, d), BF16)],
        compiler_params=_params("arbitrary"),
        name="outproj",
    )(x, mod, g_mlp, yc, yg, yw, w)


def _mlp_kernel(*refs, final_norm):
    if final_norm:
        x_ref, h_ref, mod_ref, w1_ref, w2_ref, gf_ref, o_ref = refs
    else:
        x_ref, h_ref, mod_ref, w1_ref, w2_ref, o_ref = refs
    acc_ref = o_ref
    j = pl.program_id(1)
    last = pl.num_programs(1) - 1

    def partial_sum():
        a = jnp.maximum(jnp.dot(h_ref[...], w1_ref[...], preferred_element_type=F32), 0.0)
        return jnp.dot((a * a).astype(BF16), w2_ref[...], preferred_element_type=F32)

    @pl.when(j == 0)
    def _():
        acc_ref[...] = partial_sum()

    @pl.when(jnp.logical_and(j > 0, j < last))
    def _():
        acc_ref[...] += partial_sum()

    @pl.when(j == last)
    def _():
        y = x_ref[...] + mod_ref[5:6, :] * (acc_ref[...] + partial_sum())
        if final_norm:
            y = _rms(y, gf_ref[...])
        o_ref[...] = y


def _mlp(x, h, mod, row_of_tile, w1, w2, layer, g_final):
    t_tokens, d = x.shape
    d_ff = w1.shape[2]
    tm, tf = 1024, 512
    assert d_ff // tf >= 2
    final_norm = g_final is not None
    in_specs = [
        pl.BlockSpec((tm, d), lambda i, j: (i, 0)),
        pl.BlockSpec((tm, d), lambda i, j: (i, 0)),
        pl.BlockSpec((None, 6, d), lambda i, j: (row_of_tile(i, tm), 0, 0)),
        pl.BlockSpec((None, d, tf), lambda i, j: (layer, 0, j)),
        pl.BlockSpec((None, tf, d), lambda i, j: (layer, j, 0)),
    ]
    args = [x, h, mod, w1, w2]
    if final_norm:
        in_specs.append(pl.BlockSpec((1, d), lambda i, j: (0, 0)))
        args.append(g_final)
    return pl.pallas_call(
        functools.partial(_mlp_kernel, final_norm=final_norm),
        grid=(t_tokens // tm, d_ff // tf),
        in_specs=in_specs,
        out_specs=pl.BlockSpec((tm, d), lambda i, j: (i, 0)),
        out_shape=jax.ShapeDtypeStruct((t_tokens, d), F32),
        compiler_params=_params("arbitrary", "arbitrary"),
        name="mlp_final" if final_norm else "mlp",
    )(*args)


def _rope_tables(n_tokens):
    rows = n_tokens // GRID_W
    r, col = jnp.meshgrid(jnp.arange(rows), jnp.arange(GRID_W), indexing='ij')
    r = r.reshape(-1).astype(F32)
    col = col.reshape(-1).astype(F32)
    n_freq = HEAD_DIM // 4
    inv = ROPE_THETA ** (-jnp.arange(n_freq, dtype=F32) / n_freq)
    ang_r = r[:, None] * inv
    ang_c = col[:, None] * inv
    ang = jnp.concatenate([ang_r, ang_r, ang_c, ang_c], axis=-1)
    sign = jnp.asarray(np.tile(np.repeat(np.array([-1.0, 1.0], np.float32), n_freq), 2))
    return jnp.cos(ang), jnp.sin(ang) * sign


def kernel(x_prompt, x_sample, cache_glob_k, cache_glob_v, cache_win_k, cache_win_v, c, c_ctx, w_ada, b_ada,
           g_attn, g_mlp, w_in, conv_w, conv_b, conv_ln_g, conv_ln_b, q_norm_g, k_norm_g, sink, w_out,
           w_mlp1, w_mlp2, g_final):
    batch, seq, d = x_prompt.shape
    dec_batch, dec_seq, _ = x_sample.shape
    depth = w_ada.shape[0]
    ctx_row = dec_batch
    assert dec_batch < MOD_ROWS

    cin = jnp.concatenate([c, c_ctx[None, :], jnp.zeros((MOD_ROWS - dec_batch - 1, d), F32)], axis=0)
    mods = _ada(cin, w_ada, b_ada).reshape(depth, MOD_ROWS, 6, d)
    rope_tabs = _rope_tables(dec_seq)

    w_in_b = w_in.astype(BF16)
    w_out_b = w_out.astype(BF16)
    w1_b = w_mlp1.astype(BF16)
    w2_b = w_mlp2.astype(BF16)
    caches = [a.reshape(dec_batch, depth, a.shape[2], -1).astype(BF16)
              for a in (cache_glob_k, cache_glob_v, cache_win_k, cache_win_v)]

    ctx_rows = lambda i, tm: ctx_row
    lat_rows = lambda i, tm: (i * tm) // dec_seq

    xp = x_prompt.reshape(batch * seq, d)
    xs = x_sample.reshape(dec_batch * dec_seq, d)
    new_cache = []
    for l in range(depth):
        mod = mods[l]
        row = lambda a: a[l].reshape(1, -1)
        g_fin = g_final.reshape(1, d) if l == depth - 1 else None
        conv_args = (conv_w[l], row(conv_b), row(conv_ln_g), row(conv_ln_b))

        u, qg, kg, vg, qw, kw, vw, *new_cache = _inproj(
            xp, mod, ctx_rows, row(g_attn), w_in_b, l, row(q_norm_g), row(k_norm_g), None, seq, new_cache)
        yc = _conv(u, *conv_args, seq)
        yg, yw = _attn_ctx(sink[l], qg, kg, vg, qw, kw, vw, seq)
        xp, hp = _outproj(xp, mod, ctx_rows, row(g_mlp), yc, yg, yw, w_out_b, l)
        xp = _mlp(xp, hp, mod, ctx_rows, w1_b, w2_b, l, g_fin)

        u, qg, kg, vg, qw, kw, vw = _inproj(
            xs, mod, lat_rows, row(g_attn), w_in_b, l, row(q_norm_g), row(k_norm_g), rope_tabs, dec_seq, None)
        yc = _conv(u, *conv_args, dec_seq)
        yg, yw = _attn_lat(sink[l], l, qg, kg, vg, caches[0], caches[1], qw, kw, vw, caches[2], caches[3],
                           dec_seq)
        xs, hs = _outproj(xs, mod, lat_rows, row(g_mlp), yc, yg, yw, w_out_b, l)
        xs = _mlp(xs, hs, mod, lat_rows, w1_b, w2_b, l, g_fin)

    return (xp.reshape(batch, seq, d), xs.reshape(dec_batch, dec_seq, d), *new_cache)
```

```python
import functools

import jax
import jax.numpy as jnp
import numpy as np
from jax import lax
from jax.experimental import pallas as pl
from jax.experimental.pallas import tpu as pltpu

HEAD_DIM = 128
CONV_CH = 512
CONV_WIDTH = 31
GLOB_HEADS = 6
GLOB_KV = 2
WIN_HEADS = 6
WIN_KV = 2
WINDOW = 128
GRID_W = 64
ROPE_THETA = 10000.0
NORM_EPS = 1e-6
NEG_INF = -1e30

GLOB_Q = GLOB_HEADS * HEAD_DIM
GLOB_KVW = GLOB_KV * HEAD_DIM
WIN_Q = WIN_HEADS * HEAD_DIM
WIN_KVW = WIN_KV * HEAD_DIM
GROUP = GLOB_HEADS // GLOB_KV
C_U = 0
C_QG = 2 * CONV_CH
C_KG = C_QG + GLOB_Q
C_VG = C_KG + GLOB_KVW
C_QW = C_VG + GLOB_KVW
C_KW = C_QW + WIN_Q
C_VW = C_KW + WIN_KVW
IN_WIDTH = C_VW + WIN_KVW

LOG2E = float(np.log2(np.e))
Q_PRESCALE = HEAD_DIM ** -0.5 * LOG2E

MOD_ROWS = 16
CONV_HALO = 16
SUBLANES = 8
CONV_BLOCK = 256
QSUB = 128
V7X_VMEM_LIMIT = 56 * 1024 * 1024

BF16 = jnp.bfloat16
F32 = jnp.float32


def _params(*sem):
    return pltpu.CompilerParams(dimension_semantics=sem, vmem_limit_bytes=V7X_VMEM_LIMIT)


def _sigmoid(x):
    return 1.0 / (1.0 + jnp.exp(-x))


def _rms(x, gain):
    return x * lax.rsqrt(jnp.mean(x * x, axis=-1, keepdims=True) + NORM_EPS) * gain


def _ada_kernel(c_ref, w_ref, b_ref, o_ref):
    c = c_ref[...]
    a = (c * _sigmoid(c)).astype(BF16)
    o_ref[...] = jnp.dot(a, w_ref[...].astype(BF16), preferred_element_type=F32) + b_ref[...]


def _ada(cin, w_ada, b_ada):
    depth, d, n = w_ada.shape
    tn = 1024
    return pl.pallas_call(
        _ada_kernel,
        grid=(depth, n // tn),
        in_specs=[
            pl.BlockSpec((MOD_ROWS, d), lambda l, j: (0, 0)),
            pl.BlockSpec((None, d, tn), lambda l, j: (l, 0, j)),
            pl.BlockSpec((None, 1, tn), lambda l, j: (l, 0, j)),
        ],
        out_specs=pl.BlockSpec((None, MOD_ROWS, tn), lambda l, j: (l, 0, j)),
        out_shape=jax.ShapeDtypeStruct((depth, MOD_ROWS, n), F32),
        compiler_params=_params("arbitrary", "arbitrary"),
        name="ada_mod",
    )(cin, w_ada, b_ada.reshape(depth, 1, n))


def _store_cache(ref, cols, t):
    nb, seq, kv, _ = ref.shape
    heads = range(kv)[slice(cols.start and cols.start // HEAD_DIM, cols.stop and cols.stop // HEAD_DIM)]
    for b in range(nb):
        for i, h in enumerate(heads):
            ref[b, :, h, :] = t[b * seq:(b + 1) * seq, i * HEAD_DIM:(i + 1) * HEAD_DIM]


def _project(x_ref, mod_ref, g_ref, w_ref, qn_ref, kn_ref, rope_refs, outs, caches):
    u_ref, qg_ref, kg_ref, vg_ref, qw_ref, kw_ref, vw_ref = outs
    h = _rms(x_ref[...], g_ref[...]) * (1.0 + mod_ref[1:2, :]) + mod_ref[0:1, :]
    hb = h.astype(BF16)

    def proj(c0, c1):
        return jnp.dot(hb, w_ref[:, c0:c1], preferred_element_type=F32)

    if rope_refs is not None:
        cos = rope_refs[0][...]
        sin = rope_refs[1][...]
        lane = lax.broadcasted_iota(jnp.int32, cos.shape, 1)
        take_next = ((lane // (HEAD_DIM // 4)) % 2) == 0

        def rot(t):
            r = jnp.where(take_next, pltpu.roll(t, HEAD_DIM - HEAD_DIM // 4, 1),
                          pltpu.roll(t, HEAD_DIM // 4, 1))
            return t * cos + r * sin
    else:
        def rot(t):
            return t

    def heads(p, n):
        return [(slice(i * HEAD_DIM, (i + 1) * HEAD_DIM), p[:, i * HEAD_DIM:(i + 1) * HEAD_DIM])
                for i in range(n)]

    u_ref[...] = proj(C_U, C_QG)

    qn = qn_ref[...]
    kn = kn_ref[...]
    for cols, t in heads(proj(C_QG, C_KG), GLOB_HEADS):
        qg_ref[:, cols] = (rot(_rms(t, qn)) * Q_PRESCALE).astype(BF16)
    for cols, t in heads(proj(C_KG, C_VG), GLOB_KV):
        t = _rms(t, kn)
        if caches is not None:
            _store_cache(caches[0], cols, t)
        kg_ref[:, cols] = rot(t).astype(BF16)
    p = proj(C_VG, C_QW)
    if caches is not None:
        _store_cache(caches[1], slice(None), p)
    vg_ref[...] = p.astype(BF16)
    for cols, t in heads(proj(C_QW, C_KW), WIN_HEADS):
        qw_ref[:, cols] = (rot(t) * Q_PRESCALE).astype(BF16)
    p = proj(C_KW, C_VW)
    if caches is not None:
        _store_cache(caches[2], slice(None), p)
    for cols, t in heads(p, WIN_KV):
        kw_ref[:, cols] = rot(t).astype(BF16)
    p = proj(C_VW, IN_WIDTH)
    if caches is not None:
        _store_cache(caches[3], slice(None), p)
    vw_ref[...] = p.astype(BF16)


PROJ_WIDTHS = ((C_QG - C_U, F32), (GLOB_Q, BF16), (GLOB_KVW, BF16), (GLOB_KVW, BF16),
               (WIN_Q, BF16), (WIN_KVW, BF16), (WIN_KVW, BF16))


def _inproj_kernel(*refs, rope, n_aliased):
    x_ref, mod_ref, g_ref, w_ref, qn_ref, kn_ref = refs[:6]
    pos = 6
    rope_refs = None
    if rope:
        rope_refs = refs[pos:pos + 2]
        pos += 2
    pos += n_aliased
    outs = refs[pos:pos + len(PROJ_WIDTHS)]
    caches = refs[pos + len(PROJ_WIDTHS):] or None
    _project(x_ref, mod_ref, g_ref, w_ref, qn_ref, kn_ref, rope_refs, outs, caches)


def _inproj(x, mod, row_of_tile, g, w, layer, qn, kn, rope_tabs, seq, prev_cache):
    t_tokens, d = x.shape
    tm = 512
    rope = rope_tabs is not None
    const = lambda i: (0, 0)
    in_specs = [
        pl.BlockSpec((tm, d), lambda i: (i, 0)),
        pl.BlockSpec((None, 6, d), lambda i: (row_of_tile(i, tm), 0, 0)),
        pl.BlockSpec((1, d), const),
        pl.BlockSpec((None, d, IN_WIDTH), lambda i: (layer, 0, 0), pipeline_mode=pl.Buffered(1)),
        pl.BlockSpec((1, HEAD_DIM), const),
        pl.BlockSpec((1, HEAD_DIM), const),
    ]
    args = [x, mod, g, w, qn, kn]
    if rope:
        per_seq = seq // tm
        in_specs += [pl.BlockSpec((tm, HEAD_DIM), lambda i: (i % per_seq, 0))] * 2
        args += list(rope_tabs)
    out_specs = [pl.BlockSpec((tm, wd), lambda i: (i, 0)) for wd, _ in PROJ_WIDTHS]
    out_shape = [jax.ShapeDtypeStruct((t_tokens, wd), dt) for wd, dt in PROJ_WIDTHS]
    aliases = {}
    n_aliased = 0
    if prev_cache is not None:
        nb = tm // seq
        n_aliased = len(prev_cache)
        for prev in prev_cache:
            aliases[len(args)] = len(out_shape)
            in_specs.append(pl.BlockSpec(memory_space=pl.ANY))
            args.append(prev)
            out_specs.append(pl.BlockSpec((nb, None, seq) + prev.shape[3:], lambda i: (i, layer, 0, 0, 0)))
            out_shape.append(jax.ShapeDtypeStruct(prev.shape, prev.dtype))
    return pl.pallas_call(
        functools.partial(_inproj_kernel, rope=rope, n_aliased=n_aliased),
        grid=(t_tokens // tm,),
        in_specs=in_specs,
        out_specs=out_specs,
        out_shape=out_shape,
        input_output_aliases=aliases,
        compiler_params=_params("arbitrary"),
        name="inproj_rope" if rope else "inproj_ctx",
    )(*args)


def _conv_body(u_ref, halo, w_ref, b_ref, lg_ref, lb_ref, o_ref, hp_ref, y_ref):
    rows = u_ref.shape[0]

    def glu(u):
        return u[:, :CONV_CH] * _sigmoid(u[:, CONV_CH:])

    zeros = jnp.zeros((CONV_HALO, CONV_CH), F32)
    if halo is not None:
        up_ref, un_ref, has_prev, has_next = halo
        hp_ref[0:CONV_HALO, :] = jnp.where(has_prev, glu(up_ref[...]), zeros)
        hp_ref[CONV_HALO + rows:, :] = jnp.where(has_next, glu(un_ref[...]), zeros)
    else:
        hp_ref[0:CONV_HALO, :] = zeros
        hp_ref[CONV_HALO + rows:, :] = zeros
    hp_ref[CONV_HALO:CONV_HALO + rows, :] = glu(u_ref[...])

    rchunk = 128
    base = CONV_HALO - CONV_WIDTH // 2
    groups = -(-(base + CONV_WIDTH) // SUBLANES)
    for c in range(CONV_CH // HEAD_DIM):
        cs = slice(c * HEAD_DIM, (c + 1) * HEAD_DIM)
        for r in range(rows // rchunk):
            acc = None
            for s in range(SUBLANES):
                part = None
                for q in range(groups):
                    k = q * SUBLANES + s - base
                    if 0 <= k < CONV_WIDTH:
                        r0 = r * rchunk + q * SUBLANES
                        term = hp_ref[r0:r0 + rchunk + SUBLANES, cs] * w_ref[k:k + 1, cs]
                        part = term if part is None else part + term
                part = part[s:s + rchunk]
                acc = part if acc is None else acc + part
            y_ref[r * rchunk:(r + 1) * rchunk, cs] = acc

    y = y_ref[...] + b_ref[...]
    mu = jnp.mean(y, axis=-1, keepdims=True)
    yc = y - mu
    var = jnp.mean(yc * yc, axis=-1, keepdims=True)
    z = yc * lax.rsqrt(var + NORM_EPS) * lg_ref[...] + lb_ref[...]
    o_ref[...] = (z * _sigmoid(z)).astype(BF16)


def _conv_kernel(*refs, blocks_per_seq):
    if blocks_per_seq > 1:
        u_ref, up_ref, un_ref = refs[:3]
        j = pl.program_id(0) % blocks_per_seq
        halo = (up_ref, un_ref, j > 0, j < blocks_per_seq - 1)
        rest = refs[3:]
    else:
        u_ref, halo, rest = refs[0], None, refs[1:]
    _conv_body(u_ref, halo, *rest)


def _conv(u, w, b, lg, lb, seq):
    t_tokens = u.shape[0]
    rows = CONV_BLOCK
    blocks_per_seq = seq // rows
    halo_per_block = rows // CONV_HALO
    n_halo = t_tokens // CONV_HALO
    const = lambda i: (0, 0)
    in_specs = [pl.BlockSpec((rows, 2 * CONV_CH), lambda i: (i, 0))]
    args = [u]
    if blocks_per_seq > 1:
        in_specs += [
            pl.BlockSpec((CONV_HALO, 2 * CONV_CH), lambda i: (jnp.maximum(i * halo_per_block - 1, 0), 0)),
            pl.BlockSpec((CONV_HALO, 2 * CONV_CH),
                         lambda i: (jnp.minimum((i + 1) * halo_per_block, n_halo - 1), 0)),
        ]
        args += [u, u]
    in_specs += [pl.BlockSpec((CONV_WIDTH, CONV_CH), const)] + [pl.BlockSpec((1, CONV_CH), const)] * 3
    args += [w, b, lg, lb]
    return pl.pallas_call(
        functools.partial(_conv_kernel, blocks_per_seq=blocks_per_seq),
        grid=(t_tokens // rows,),
        in_specs=in_specs,
        out_specs=pl.BlockSpec((rows, CONV_CH), lambda i: (i, 0)),
        out_shape=jax.ShapeDtypeStruct((t_tokens, CONV_CH), BF16),
        scratch_shapes=[pltpu.VMEM((rows + 2 * CONV_HALO, CONV_CH), F32),
                        pltpu.VMEM((rows, CONV_CH), F32)],
        compiler_params=_params("arbitrary"),
        name="conv_mixer",
    )(*args)


def _stack_heads(q, g):
    return jnp.concatenate(
        [q[:, (g * GROUP + j) * HEAD_DIM:(g * GROUP + j + 1) * HEAD_DIM] for j in range(GROUP)], axis=0)


def _scores(q, k):
    return lax.dot_general(q, k, (((1,), (1,)), ((), ())), preferred_element_type=F32)


def _sink_column(sink_ref, g, rows):
    return jnp.concatenate(
        [jnp.full((rows, 1), sink_ref[g * GROUP + j] * LOG2E, F32) for j in range(GROUP)], axis=0)


def _store_heads(o_ref, o, g, rows):
    for j in range(GROUP):
        c0 = (g * GROUP + j) * HEAD_DIM
        o_ref[:, c0:c0 + HEAD_DIM] = o[j * rows:(j + 1) * rows].astype(o_ref.dtype)


def _attention_jobs(jobs):
    n = len(jobs)
    pieces = [None] * n
    soft = [None] * n
    for t in range(n + 2):
        if t < n:
            pieces[t] = jobs[t][0]()
        if 0 <= t - 2 < n:
            es, rden = soft[t - 2]
            o = functools.reduce(jnp.add, [jnp.dot(e, v, preferred_element_type=F32)
                                           for e, (_, v) in zip(es, pieces[t - 2])])
            jobs[t - 2][2](o * rden)
        if 0 <= t - 1 < n:
            snk = jobs[t - 1][1]
            m = functools.reduce(jnp.maximum, [jnp.max(s, axis=-1, keepdims=True) for s, _ in pieces[t - 1]])
            if snk is not None:
                m = jnp.maximum(m, snk)
            es = [jnp.exp2(s - m) for s, _ in pieces[t - 1]]
            den = functools.reduce(jnp.add, [jnp.sum(e, axis=-1, keepdims=True) for e in es])
            if snk is not None:
                den = den + jnp.exp2(snk - m)
            soft[t - 1] = ([e.astype(BF16) for e in es], 1.0 / den)


def _attn_ctx_kernel(sink_ref, qg_ref, kg_ref, vg_ref, qw_ref, kw_ref, vw_ref, og_ref, ow_ref, *, seq):
    jobs = []
    for b in range(qg_ref.shape[0] // seq):
        rs = slice(b * seq, (b + 1) * seq)
        for g in range(GLOB_KV):
            hs = slice(g * HEAD_DIM, (g + 1) * HEAD_DIM)

            def glob_scores(rs=rs, hs=hs, g=g):
                return [(_scores(_stack_heads(qg_ref[rs, :], g), kg_ref[rs, hs]), vg_ref[rs, hs])]

            def win_scores(rs=rs, hs=hs, g=g):
                return [(_scores(_stack_heads(qw_ref[rs, :], g), kw_ref[rs, hs]), vw_ref[rs, hs])]

            jobs.append((glob_scores, None, functools.partial(_store_heads, og_ref.at[rs, :], g=g, rows=seq)))
            jobs.append((win_scores, _sink_column(sink_ref, g, seq),
                         functools.partial(_store_heads, ow_ref.at[rs, :], g=g, rows=seq)))
    _attention_jobs(jobs)


def _attn_ctx(sink, qg, kg, vg, qw, kw, vw, seq):
    t_tokens = qg.shape[0]
    rows = 2 * seq
    spec = lambda wd: pl.BlockSpec((rows, wd), lambda i: (i, 0))
    return pl.pallas_call(
        functools.partial(_attn_ctx_kernel, seq=seq),
        grid=(t_tokens // rows,),
        in_specs=[pl.BlockSpec(memory_space=pltpu.SMEM),
                  spec(GLOB_Q), spec(GLOB_KVW), spec(GLOB_KVW), spec(WIN_Q), spec(WIN_KVW), spec(WIN_KVW)],
        out_specs=[spec(GLOB_Q), spec(WIN_Q)],
        out_shape=[jax.ShapeDtypeStruct((t_tokens, GLOB_Q), BF16),
                   jax.ShapeDtypeStruct((t_tokens, WIN_Q), BF16)],
        compiler_params=_params("arbitrary"),
        name="attn_ctx",
    )(sink, qg, kg, vg, qw, kw, vw)


def _attn_lat_kernel(sink_ref, qg_ref, kg_ref, vg_ref, cgk_ref, cgv_ref,
                     qw_ref, kw_ref, vw_ref, cwk_ref, cwv_ref, og_ref, ow_ref,
                     cgk_b, cgv_b, cwk_b, cwv_b, *, seq, qb):
    bi = pl.program_id(1)

    @pl.when(bi == 0)
    def _():
        for src, dst in ((cgk_ref, cgk_b), (cgv_ref, cgv_b), (cwk_ref, cwk_b), (cwv_ref, cwv_b)):
            for h in range(src.shape[1]):
                dst[:, h * HEAD_DIM:(h + 1) * HEAD_DIM] = src[:, h, :].astype(BF16)

    span = QSUB + 2 * WINDOW
    jobs = []
    for sb in range(qb // QSUB):
        rs = slice(sb * QSUB, (sb + 1) * QSUB)
        q0 = bi * qb + sb * QSUB
        start = pl.multiple_of(jnp.clip(q0 - WINDOW, 0, seq - span), HEAD_DIM)
        qpos = q0 + lax.broadcasted_iota(jnp.int32, (QSUB, span), 0)
        kpos = start + lax.broadcasted_iota(jnp.int32, (QSUB, span), 1)
        band = jnp.where(jnp.abs(qpos - kpos) <= WINDOW, 0.0, NEG_INF)
        band = jnp.concatenate([band] * GROUP, axis=0)
        for g in range(GLOB_KV):
            hs = slice(g * HEAD_DIM, (g + 1) * HEAD_DIM)

            def glob_scores(rs=rs, hs=hs, g=g):
                q = _stack_heads(qg_ref[rs, :], g)
                return [(_scores(q, cgk_b[:, hs]), cgv_b[:, hs]), (_scores(q, kg_ref[:, hs]), vg_ref[:, hs])]

            def win_scores(rs=rs, hs=hs, g=g, start=start, band=band):
                q = _stack_heads(qw_ref[rs, :], g)
                s_loc = _scores(q, kw_ref[pl.ds(start, span), hs]) + band
                return [(_scores(q, cwk_b[:, hs]), cwv_b[:, hs]), (s_loc, vw_ref[pl.ds(start, span), hs])]

            jobs.append((glob_scores, None, functools.partial(_store_heads, og_ref.at[rs, :], g=g, rows=QSUB)))
            jobs.append((win_scores, _sink_column(sink_ref, g, QSUB),
                         functools.partial(_store_heads, ow_ref.at[rs, :], g=g, rows=QSUB)))
    _attention_jobs(jobs)


def _attn_lat(sink, layer, qg, kg, vg, cgk, cgv, qw, kw, vw, cwk, cwv, seq):
    t_tokens = qg.shape[0]
    nbatch = t_tokens // seq
    qb = QSUB
    nqb = seq // qb
    past = cgk.shape[2]
    qspec = lambda wd: pl.BlockSpec((qb, wd), lambda b, i: (b * nqb + i, 0))
    kvspec = lambda wd: pl.BlockSpec((seq, wd), lambda b, i: (b, 0))
    cspec = lambda wd: pl.BlockSpec((None, None, past, wd // HEAD_DIM, HEAD_DIM), lambda b, i: (b, layer, 0, 0, 0))
    return pl.pallas_call(
        functools.partial(_attn_lat_kernel, seq=seq, qb=qb),
        grid=(nbatch, nqb),
        in_specs=[pl.BlockSpec(memory_space=pltpu.SMEM),
                  qspec(GLOB_Q), kvspec(GLOB_KVW), kvspec(GLOB_KVW), cspec(GLOB_KVW), cspec(GLOB_KVW),
                  qspec(WIN_Q), kvspec(WIN_KVW), kvspec(WIN_KVW), cspec(WIN_KVW), cspec(WIN_KVW)],
        out_specs=[qspec(GLOB_Q), qspec(WIN_Q)],
        out_shape=[jax.ShapeDtypeStruct((t_tokens, GLOB_Q), BF16),
                   jax.ShapeDtypeStruct((t_tokens, WIN_Q), BF16)],
        scratch_shapes=[pltpu.VMEM((past, wd), BF16) for wd in (GLOB_KVW, GLOB_KVW, WIN_KVW, WIN_KVW)],
        compiler_params=_params("arbitrary", "arbitrary"),
        name="attn_lat",
    )(sink, qg, kg, vg, cgk, cgv, qw, kw, vw, cwk, cwv)


def _outproj_kernel(x_ref, mod_ref, g_ref, yc_ref, yg_ref, yw_ref, w_ref, o_ref, h_ref, *, rchunk):
    c1 = CONV_CH
    c2 = CONV_CH + GLOB_Q
    for r in range(x_ref.shape[0] // rchunk):
        rs = slice(r * rchunk, (r + 1) * rchunk)
        mix = (jnp.dot(yc_ref[rs, :], w_ref[0:c1, :], preferred_element_type=F32)
               + jnp.dot(yg_ref[rs, :], w_ref[c1:c2, :], preferred_element_type=F32)
               + jnp.dot(yw_ref[rs, :], w_ref[c2:, :], preferred_element_type=F32))
        x1 = x_ref[rs, :] + mod_ref[2:3, :] * mix
        o_ref[rs, :] = x1
        h = _rms(x1, g_ref[...]) * (1.0 + mod_ref[4:5, :]) + mod_ref[3:4, :]
        h_ref[rs, :] = h.astype(BF16)


def _outproj(x, mod, row_of_tile, g_mlp, yc, yg, yw, w, layer):
    t_tokens, d = x.shape
    tm = 512
    row = lambda wd: pl.BlockSpec((tm, wd), lambda i: (i, 0))
    return pl.pallas_call(
        functools.partial(_outproj_kernel, rchunk=256),
        grid=(t_tokens // tm,),
        in_specs=[row(d),
                  pl.BlockSpec((None, 6, d), lambda i: (row_of_tile(i, tm), 0, 0)),
                  pl.BlockSpec((1, d), lambda i: (0, 0)),
                  row(CONV_CH), row(GLOB_Q), row(WIN_Q),
                  pl.BlockSpec((None,) + w.shape[1:], lambda i: (layer, 0, 0), pipeline_mode=pl.Buffered(1))],
        out_specs=[row(d), row(d)],
        out_shape=[jax.ShapeDtypeStruct((t_tokens, d), F32), jax.ShapeDtypeStruct((t_tokens, d), BF16)],
        compiler_params=_params("arbitrary"),
        name="outproj",
    )(x, mod, g_mlp, yc, yg, yw, w)


def _mlp_kernel(*refs, final_norm):
    if final_norm:
        x_ref, h_ref, mod_ref, w1_ref, w2_ref, gf_ref, o_ref = refs
    else:
        x_ref, h_ref, mod_ref, w1_ref, w2_ref, o_ref = refs
    acc_ref = o_ref
    j = pl.program_id(1)
    last = pl.num_programs(1) - 1

    def partial_sum():
        a = jnp.maximum(jnp.dot(h_ref[...], w1_ref[...], preferred_element_type=F32), 0.0)
        return jnp.dot((a * a).astype(BF16), w2_ref[...], preferred_element_type=F32)

    @pl.when(j == 0)
    def _():
        acc_ref[...] = partial_sum()

    @pl.when(jnp.logical_and(j > 0, j < last))
    def _():
        acc_ref[...] += partial_sum()

    @pl.when(j == last)
    def _():
        y = x_ref[...] + mod_ref[5:6, :] * (acc_ref[...] + partial_sum())
        if final_norm:
            y = _rms(y, gf_ref[...])
        o_ref[...] = y


def _mlp(x, h, mod, row_of_tile, w1, w2, layer, g_final):
    t_tokens, d = x.shape
    d_ff = w1.shape[2]
    tm, tf = 512, 1024
    assert d_ff // tf >= 2
    final_norm = g_final is not None
    in_specs = [
        pl.BlockSpec((tm, d), lambda i, j: (i, 0)),
        pl.BlockSpec((tm, d), lambda i, j: (i, 0)),
        pl.BlockSpec((None, 6, d), lambda i, j: (row_of_tile(i, tm), 0, 0)),
        pl.BlockSpec((None, d, tf), lambda i, j: (layer, 0, j)),
        pl.BlockSpec((None, tf, d), lambda i, j: (layer, j, 0)),
    ]
    args = [x, h, mod, w1, w2]
    if final_norm:
        in_specs.append(pl.BlockSpec((1, d), lambda i, j: (0, 0)))
        args.append(g_final)
    return pl.pallas_call(
        functools.partial(_mlp_kernel, final_norm=final_norm),
        grid=(t_tokens // tm, d_ff // tf),
        in_specs=in_specs,
        out_specs=pl.BlockSpec((tm, d), lambda i, j: (i, 0)),
        out_shape=jax.ShapeDtypeStruct((t_tokens, d), F32),
        compiler_params=_params("arbitrary", "arbitrary"),
        name="mlp_final" if final_norm else "mlp",
    )(*args)


def _rope_tables(n_tokens):
    rows = n_tokens // GRID_W
    r, col = jnp.meshgrid(jnp.arange(rows), jnp.arange(GRID_W), indexing='ij')
    r = r.reshape(-1).astype(F32)
    col = col.reshape(-1).astype(F32)
    n_freq = HEAD_DIM // 4
    inv = ROPE_THETA ** (-jnp.arange(n_freq, dtype=F32) / n_freq)
    ang_r = r[:, None] * inv
    ang_c = col[:, None] * inv
    ang = jnp.concatenate([ang_r, ang_r, ang_c, ang_c], axis=-1)
    sign = jnp.asarray(np.tile(np.repeat(np.array([-1.0, 1.0], np.float32), n_freq), 2))
    return jnp.cos(ang), jnp.sin(ang) * sign


def kernel(x_prompt, x_sample, cache_glob_k, cache_glob_v, cache_win_k, cache_win_v, c, c_ctx, w_ada, b_ada,
           g_attn, g_mlp, w_in, conv_w, conv_b, conv_ln_g, conv_ln_b, q_norm_g, k_norm_g, sink, w_out,
           w_mlp1, w_mlp2, g_final):
    batch, seq, d = x_prompt.shape
    dec_batch, dec_seq, _ = x_sample.shape
    depth = w_ada.shape[0]
    ctx_row = dec_batch
    assert dec_batch < MOD_ROWS

    cin = jnp.concatenate([c, c_ctx[None, :], jnp.zeros((MOD_ROWS - dec_batch - 1, d), F32)], axis=0)
    mods = _ada(cin, w_ada, b_ada).reshape(depth, MOD_ROWS, 6, d)
    rope_tabs = _rope_tables(dec_seq)

    w_in_b = w_in.astype(BF16)
    w_out_b = w_out.astype(BF16)
    w1_b = w_mlp1.astype(BF16)
    w2_b = w_mlp2.astype(BF16)
    caches = (cache_glob_k, cache_glob_v, cache_win_k, cache_win_v)

    ctx_rows = lambda i, tm: ctx_row
    lat_rows = lambda i, tm: (i * tm) // dec_seq

    xp = x_prompt.reshape(batch * seq, d)
    xs = x_sample.reshape(dec_batch * dec_seq, d)
    new_cache = [jnp.zeros((batch, depth, seq, kv, HEAD_DIM), F32) for kv in (GLOB_KV, GLOB_KV, WIN_KV, WIN_KV)]
    for l in range(depth):
        mod = mods[l]
        row = lambda a: a[l].reshape(1, -1)
        g_fin = g_final.reshape(1, d) if l == depth - 1 else None
        conv_args = (conv_w[l], row(conv_b), row(conv_ln_g), row(conv_ln_b))

        u, qg, kg, vg, qw, kw, vw, *new_cache = _inproj(
            xp, mod, ctx_rows, row(g_attn), w_in_b, l, row(q_norm_g), row(k_norm_g), None, seq, new_cache)
        yc = _conv(u, *conv_args, seq)
        yg, yw = _attn_ctx(sink[l], qg, kg, vg, qw, kw, vw, seq)
        xp, hp = _outproj(xp, mod, ctx_rows, row(g_mlp), yc, yg, yw, w_out_b, l)
        xp = _mlp(xp, hp, mod, ctx_rows, w1_b, w2_b, l, g_fin)

        u, qg, kg, vg, qw, kw, vw = _inproj(
            xs, mod, lat_rows, row(g_attn), w_in_b, l, row(q_norm_g), row(k_norm_g), rope_tabs, dec_seq, None)
        yc = _conv(u, *conv_args, dec_seq)
        yg, yw = _attn_lat(sink[l], l, qg, kg, vg, caches[0], caches[1], qw, kw, vw, caches[2], caches[3],
                           dec_seq)
        xs, hs = _outproj(xs, mod, lat_rows, row(g_mlp), yc, yg, yw, w_out_b, l)
        xs = _mlp(xs, hs, mod, lat_rows, w1_b, w2_b, l, g_fin)

    return (xp.reshape(batch, seq, d), xs.reshape(dec_batch, dec_seq, d), *new_cache)
```

```python
import functools

import jax
import jax.numpy as jnp
import numpy as np
from jax import lax
from jax.experimental import pallas as pl
from jax.experimental.pallas import tpu as pltpu

HEAD_DIM = 128
CONV_CH = 512
CONV_WIDTH = 31
GLOB_HEADS = 6
GLOB_KV = 2
WIN_HEADS = 6
WIN_KV = 2
WINDOW = 128
GRID_W = 64
ROPE_THETA = 10000.0
NORM_EPS = 1e-6
NEG_INF = -1e30

GLOB_Q = GLOB_HEADS * HEAD_DIM
GLOB_KVW = GLOB_KV * HEAD_DIM
WIN_Q = WIN_HEADS * HEAD_DIM
WIN_KVW = WIN_KV * HEAD_DIM
GROUP = GLOB_HEADS // GLOB_KV
C_U = 0
C_QG = 2 * CONV_CH
C_KG = C_QG + GLOB_Q
C_VG = C_KG + GLOB_KVW
C_QW = C_VG + GLOB_KVW
C_KW = C_QW + WIN_Q
C_VW = C_KW + WIN_KVW
IN_WIDTH = C_VW + WIN_KVW

LOG2E = float(np.log2(np.e))
Q_PRESCALE = HEAD_DIM ** -0.5 * LOG2E

MOD_ROWS = 16
CONV_HALO = 16
SUBLANES = 8
CONV_BLOCK = 256
QSUB = 128
V7X_VMEM_LIMIT = 56 * 1024 * 1024

BF16 = jnp.bfloat16
F32 = jnp.float32


def _params(*sem):
    return pltpu.CompilerParams(dimension_semantics=sem, vmem_limit_bytes=V7X_VMEM_LIMIT)


def _sigmoid(x):
    return 1.0 / (1.0 + jnp.exp(-x))


def _rms(x, gain):
    return x * lax.rsqrt(jnp.mean(x * x, axis=-1, keepdims=True) + NORM_EPS) * gain


def _ada_kernel(c_ref, w_ref, b_ref, o_ref):
    c = c_ref[...]
    a = (c * _sigmoid(c)).astype(BF16)
    o_ref[...] = jnp.dot(a, w_ref[...].astype(BF16), preferred_element_type=F32) + b_ref[...]


def _ada(cin, w_ada, b_ada):
    depth, d, n = w_ada.shape
    tn = 1024
    return pl.pallas_call(
        _ada_kernel,
        grid=(depth, n // tn),
        in_specs=[
            pl.BlockSpec((MOD_ROWS, d), lambda l, j: (0, 0)),
            pl.BlockSpec((None, d, tn), lambda l, j: (l, 0, j)),
            pl.BlockSpec((None, 1, tn), lambda l, j: (l, 0, j)),
        ],
        out_specs=pl.BlockSpec((None, MOD_ROWS, tn), lambda l, j: (l, 0, j)),
        out_shape=jax.ShapeDtypeStruct((depth, MOD_ROWS, n), F32),
        compiler_params=_params("arbitrary", "arbitrary"),
        name="ada_mod",
    )(cin, w_ada, b_ada.reshape(depth, 1, n))


def _store_cache(ref, cols, t):
    nb, seq, kv, _ = ref.shape
    heads = range(kv)[slice(cols.start and cols.start // HEAD_DIM, cols.stop and cols.stop // HEAD_DIM)]
    for b in range(nb):
        for i, h in enumerate(heads):
            ref[b, :, h, :] = t[b * seq:(b + 1) * seq, i * HEAD_DIM:(i + 1) * HEAD_DIM]


def _project(x_ref, mod_ref, g_ref, w_ref, qn_ref, kn_ref, rope_refs, outs, caches):
    u_ref, qg_ref, kg_ref, vg_ref, qw_ref, kw_ref, vw_ref = outs
    h = _rms(x_ref[...], g_ref[...]) * (1.0 + mod_ref[1:2, :]) + mod_ref[0:1, :]
    hb = h.astype(BF16)

    def proj(c0, c1):
        return jnp.dot(hb, w_ref[:, c0:c1], preferred_element_type=F32)

    if rope_refs is not None:
        cos = rope_refs[0][...]
        sin = rope_refs[1][...]
        lane = lax.broadcasted_iota(jnp.int32, cos.shape, 1)
        take_next = ((lane // (HEAD_DIM // 4)) % 2) == 0

        def rot(t):
            r = jnp.where(take_next, pltpu.roll(t, HEAD_DIM - HEAD_DIM // 4, 1),
                          pltpu.roll(t, HEAD_DIM // 4, 1))
            return t * cos + r * sin
    else:
        def rot(t):
            return t

    def heads(p, n):
        return [(slice(i * HEAD_DIM, (i + 1) * HEAD_DIM), p[:, i * HEAD_DIM:(i + 1) * HEAD_DIM])
                for i in range(n)]

    u_ref[...] = proj(C_U, C_QG)

    qn = qn_ref[...]
    kn = kn_ref[...]
    for cols, t in heads(proj(C_QG, C_KG), GLOB_HEADS):
        qg_ref[:, cols] = (rot(_rms(t, qn)) * Q_PRESCALE).astype(BF16)
    for cols, t in heads(proj(C_KG, C_VG), GLOB_KV):
        t = _rms(t, kn)
        if caches is not None:
            _store_cache(caches[0], cols, t)
        kg_ref[:, cols] = rot(t).astype(BF16)
    p = proj(C_VG, C_QW)
    if caches is not None:
        _store_cache(caches[1], slice(None), p)
    vg_ref[...] = p.astype(BF16)
    for cols, t in heads(proj(C_QW, C_KW), WIN_HEADS):
        qw_ref[:, cols] = (rot(t) * Q_PRESCALE).astype(BF16)
    p = proj(C_KW, C_VW)
    if caches is not None:
        _store_cache(caches[2], slice(None), p)
    for cols, t in heads(p, WIN_KV):
        kw_ref[:, cols] = rot(t).astype(BF16)
    p = proj(C_VW, IN_WIDTH)
    if caches is not None:
        _store_cache(caches[3], slice(None), p)
    vw_ref[...] = p.astype(BF16)


PROJ_WIDTHS = ((C_QG - C_U, F32), (GLOB_Q, BF16), (GLOB_KVW, BF16), (GLOB_KVW, BF16),
               (WIN_Q, BF16), (WIN_KVW, BF16), (WIN_KVW, BF16))


def _inproj_kernel(*refs, rope, n_aliased):
    x_ref, mod_ref, g_ref, w_ref, qn_ref, kn_ref = refs[:6]
    pos = 6
    rope_refs = None
    if rope:
        rope_refs = refs[pos:pos + 2]
        pos += 2
    pos += n_aliased
    outs = refs[pos:pos + len(PROJ_WIDTHS)]
    caches = refs[pos + len(PROJ_WIDTHS):] or None
    _project(x_ref, mod_ref, g_ref, w_ref, qn_ref, kn_ref, rope_refs, outs, caches)


def _inproj(x, mod, row_of_tile, g, w, layer, qn, kn, rope_tabs, seq, prev_cache):
    t_tokens, d = x.shape
    tm = 512
    rope = rope_tabs is not None
    const = lambda i: (0, 0)
    in_specs = [
        pl.BlockSpec((tm, d), lambda i: (i, 0)),
        pl.BlockSpec((None, 6, d), lambda i: (row_of_tile(i, tm), 0, 0)),
        pl.BlockSpec((1, d), const),
        pl.BlockSpec((d, IN_WIDTH), const, pipeline_mode=pl.Buffered(1)),
        pl.BlockSpec((1, HEAD_DIM), const),
        pl.BlockSpec((1, HEAD_DIM), const),
    ]
    args = [x, mod, g, w, qn, kn]
    if rope:
        per_seq = seq // tm
        in_specs += [pl.BlockSpec((tm, HEAD_DIM), lambda i: (i % per_seq, 0))] * 2
        args += list(rope_tabs)
    out_specs = [pl.BlockSpec((tm, wd), lambda i: (i, 0)) for wd, _ in PROJ_WIDTHS]
    out_shape = [jax.ShapeDtypeStruct((t_tokens, wd), dt) for wd, dt in PROJ_WIDTHS]
    aliases = {}
    n_aliased = 0
    if prev_cache is not None:
        nb = tm // seq
        n_aliased = len(prev_cache)
        for prev in prev_cache:
            aliases[len(args)] = len(out_shape)
            in_specs.append(pl.BlockSpec(memory_space=pl.ANY))
            args.append(prev)
            out_specs.append(pl.BlockSpec((nb, None, seq) + prev.shape[3:], lambda i: (i, layer, 0, 0, 0)))
            out_shape.append(jax.ShapeDtypeStruct(prev.shape, prev.dtype))
    return pl.pallas_call(
        functools.partial(_inproj_kernel, rope=rope, n_aliased=n_aliased),
        grid=(t_tokens // tm,),
        in_specs=in_specs,
        out_specs=out_specs,
        out_shape=out_shape,
        input_output_aliases=aliases,
        compiler_params=_params("arbitrary"),
        name="inproj_rope" if rope else "inproj_ctx",
    )(*args)


def _conv_body(u_ref, halo, w_ref, b_ref, lg_ref, lb_ref, o_ref, hp_ref, y_ref):
    rows = u_ref.shape[0]

    def glu(u):
        return u[:, :CONV_CH] * _sigmoid(u[:, CONV_CH:])

    zeros = jnp.zeros((CONV_HALO, CONV_CH), F32)
    if halo is not None:
        up_ref, un_ref, has_prev, has_next = halo
        hp_ref[0:CONV_HALO, :] = jnp.where(has_prev, glu(up_ref[...]), zeros)
        hp_ref[CONV_HALO + rows:, :] = jnp.where(has_next, glu(un_ref[...]), zeros)
    else:
        hp_ref[0:CONV_HALO, :] = zeros
        hp_ref[CONV_HALO + rows:, :] = zeros
    hp_ref[CONV_HALO:CONV_HALO + rows, :] = glu(u_ref[...])

    rchunk = 128
    base = CONV_HALO - CONV_WIDTH // 2
    groups = -(-(base + CONV_WIDTH) // SUBLANES)
    for c in range(CONV_CH // HEAD_DIM):
        cs = slice(c * HEAD_DIM, (c + 1) * HEAD_DIM)
        for r in range(rows // rchunk):
            acc = None
            for s in range(SUBLANES):
                part = None
                for q in range(groups):
                    k = q * SUBLANES + s - base
                    if 0 <= k < CONV_WIDTH:
                        r0 = r * rchunk + q * SUBLANES
                        term = hp_ref[r0:r0 + rchunk + SUBLANES, cs] * w_ref[k:k + 1, cs]
                        part = term if part is None else part + term
                part = part[s:s + rchunk]
                acc = part if acc is None else acc + part
            y_ref[r * rchunk:(r + 1) * rchunk, cs] = acc

    y = y_ref[...] + b_ref[...]
    mu = jnp.mean(y, axis=-1, keepdims=True)
    yc = y - mu
    var = jnp.mean(yc * yc, axis=-1, keepdims=True)
    z = yc * lax.rsqrt(var + NORM_EPS) * lg_ref[...] + lb_ref[...]
    o_ref[...] = (z * _sigmoid(z)).astype(BF16)


def _conv_kernel(*refs, blocks_per_seq):
    if blocks_per_seq > 1:
        u_ref, up_ref, un_ref = refs[:3]
        j = pl.program_id(0) % blocks_per_seq
        halo = (up_ref, un_ref, j > 0, j < blocks_per_seq - 1)
        rest = refs[3:]
    else:
        u_ref, halo, rest = refs[0], None, refs[1:]
    _conv_body(u_ref, halo, *rest)


def _conv(u, w, b, lg, lb, seq):
    t_tokens = u.shape[0]
    rows = CONV_BLOCK
    blocks_per_seq = seq // rows
    halo_per_block = rows // CONV_HALO
    n_halo = t_tokens // CONV_HALO
    const = lambda i: (0, 0)
    in_specs = [pl.BlockSpec((rows, 2 * CONV_CH), lambda i: (i, 0))]
    args = [u]
    if blocks_per_seq > 1:
        in_specs += [
            pl.BlockSpec((CONV_HALO, 2 * CONV_CH), lambda i: (jnp.maximum(i * halo_per_block - 1, 0), 0)),
            pl.BlockSpec((CONV_HALO, 2 * CONV_CH),
                         lambda i: (jnp.minimum((i + 1) * halo_per_block, n_halo - 1), 0)),
        ]
        args += [u, u]
    in_specs += [pl.BlockSpec((CONV_WIDTH, CONV_CH), const)] + [pl.BlockSpec((1, CONV_CH), const)] * 3
    args += [w, b, lg, lb]
    return pl.pallas_call(
        functools.partial(_conv_kernel, blocks_per_seq=blocks_per_seq),
        grid=(t_tokens // rows,),
        in_specs=in_specs,
        out_specs=pl.BlockSpec((rows, CONV_CH), lambda i: (i, 0)),
        out_shape=jax.ShapeDtypeStruct((t_tokens, CONV_CH), BF16),
        scratch_shapes=[pltpu.VMEM((rows + 2 * CONV_HALO, CONV_CH), F32),
                        pltpu.VMEM((rows, CONV_CH), F32)],
        compiler_params=_params("arbitrary"),
        name="conv_mixer",
    )(*args)


def _stack_heads(q, g):
    return jnp.concatenate(
        [q[:, (g * GROUP + j) * HEAD_DIM:(g * GROUP + j + 1) * HEAD_DIM] for j in range(GROUP)], axis=0)


def _scores(q, k):
    return lax.dot_general(q, k, (((1,), (1,)), ((), ())), preferred_element_type=F32)


def _sink_column(sink_ref, g, rows):
    return jnp.concatenate(
        [jnp.full((rows, 1), sink_ref[g * GROUP + j] * LOG2E, F32) for j in range(GROUP)], axis=0)


def _store_heads(o_ref, o, g, rows):
    for j in range(GROUP):
        c0 = (g * GROUP + j) * HEAD_DIM
        o_ref[:, c0:c0 + HEAD_DIM] = o[j * rows:(j + 1) * rows].astype(o_ref.dtype)


def _attention_jobs(jobs):
    n = len(jobs)
    pieces = [None] * n
    soft = [None] * n
    for t in range(n + 2):
        if t < n:
            pieces[t] = jobs[t][0]()
        if 0 <= t - 2 < n:
            es, rden = soft[t - 2]
            o = functools.reduce(jnp.add, [jnp.dot(e, v, preferred_element_type=F32)
                                           for e, (_, v) in zip(es, pieces[t - 2])])
            jobs[t - 2][2](o * rden)
        if 0 <= t - 1 < n:
            snk = jobs[t - 1][1]
            m = functools.reduce(jnp.maximum, [jnp.max(s, axis=-1, keepdims=True) for s, _ in pieces[t - 1]])
            if snk is not None:
                m = jnp.maximum(m, snk)
            es = [jnp.exp2(s - m) for s, _ in pieces[t - 1]]
            den = functools.reduce(jnp.add, [jnp.sum(e, axis=-1, keepdims=True) for e in es])
            if snk is not None:
                den = den + jnp.exp2(snk - m)
            soft[t - 1] = ([e.astype(BF16) for e in es], 1.0 / den)


def _attn_ctx_kernel(sink_ref, qg_ref, kg_ref, vg_ref, qw_ref, kw_ref, vw_ref, og_ref, ow_ref, *, seq):
    jobs = []
    for b in range(qg_ref.shape[0] // seq):
        rs = slice(b * seq, (b + 1) * seq)
        for g in range(GLOB_KV):
            hs = slice(g * HEAD_DIM, (g + 1) * HEAD_DIM)

            def glob_scores(rs=rs, hs=hs, g=g):
                return [(_scores(_stack_heads(qg_ref[rs, :], g), kg_ref[rs, hs]), vg_ref[rs, hs])]

            def win_scores(rs=rs, hs=hs, g=g):
                return [(_scores(_stack_heads(qw_ref[rs, :], g), kw_ref[rs, hs]), vw_ref[rs, hs])]

            jobs.append((glob_scores, None, functools.partial(_store_heads, og_ref.at[rs, :], g=g, rows=seq)))
            jobs.append((win_scores, _sink_column(sink_ref, g, seq),
                         functools.partial(_store_heads, ow_ref.at[rs, :], g=g, rows=seq)))
    _attention_jobs(jobs)


def _attn_ctx(sink, qg, kg, vg, qw, kw, vw, seq):
    t_tokens = qg.shape[0]
    rows = 2 * seq
    spec = lambda wd: pl.BlockSpec((rows, wd), lambda i: (i, 0))
    return pl.pallas_call(
        functools.partial(_attn_ctx_kernel, seq=seq),
        grid=(t_tokens // rows,),
        in_specs=[pl.BlockSpec(memory_space=pltpu.SMEM),
                  spec(GLOB_Q), spec(GLOB_KVW), spec(GLOB_KVW), spec(WIN_Q), spec(WIN_KVW), spec(WIN_KVW)],
        out_specs=[spec(GLOB_Q), spec(WIN_Q)],
        out_shape=[jax.ShapeDtypeStruct((t_tokens, GLOB_Q), BF16),
                   jax.ShapeDtypeStruct((t_tokens, WIN_Q), BF16)],
        compiler_params=_params("arbitrary"),
        name="attn_ctx",
    )(sink, qg, kg, vg, qw, kw, vw)


def _attn_lat_kernel(sink_ref, qg_ref, kg_ref, vg_ref, cgk_ref, cgv_ref,
                     qw_ref, kw_ref, vw_ref, cwk_ref, cwv_ref, og_ref, ow_ref,
                     cgk_b, cgv_b, cwk_b, cwv_b, *, seq, qb):
    bi = pl.program_id(1)

    @pl.when(bi == 0)
    def _():
        for src, dst in ((cgk_ref, cgk_b), (cgv_ref, cgv_b), (cwk_ref, cwk_b), (cwv_ref, cwv_b)):
            for h in range(src.shape[1]):
                dst[:, h * HEAD_DIM:(h + 1) * HEAD_DIM] = src[:, h, :].astype(BF16)

    span = QSUB + 2 * WINDOW
    jobs = []
    for sb in range(qb // QSUB):
        rs = slice(sb * QSUB, (sb + 1) * QSUB)
        q0 = bi * qb + sb * QSUB
        start = pl.multiple_of(jnp.clip(q0 - WINDOW, 0, seq - span), HEAD_DIM)
        qpos = q0 + lax.broadcasted_iota(jnp.int32, (QSUB, span), 0)
        kpos = start + lax.broadcasted_iota(jnp.int32, (QSUB, span), 1)
        band = jnp.where(jnp.abs(qpos - kpos) <= WINDOW, 0.0, NEG_INF)
        band = jnp.concatenate([band] * GROUP, axis=0)
        for g in range(GLOB_KV):
            hs = slice(g * HEAD_DIM, (g + 1) * HEAD_DIM)

            def glob_scores(rs=rs, hs=hs, g=g):
                q = _stack_heads(qg_ref[rs, :], g)
                return [(_scores(q, cgk_b[:, hs]), cgv_b[:, hs]), (_scores(q, kg_ref[:, hs]), vg_ref[:, hs])]

            def win_scores(rs=rs, hs=hs, g=g, start=start, band=band):
                q = _stack_heads(qw_ref[rs, :], g)
                s_loc = _scores(q, kw_ref[pl.ds(start, span), hs]) + band
                return [(_scores(q, cwk_b[:, hs]), cwv_b[:, hs]), (s_loc, vw_ref[pl.ds(start, span), hs])]

            jobs.append((glob_scores, None, functools.partial(_store_heads, og_ref.at[rs, :], g=g, rows=QSUB)))
            jobs.append((win_scores, _sink_column(sink_ref, g, QSUB),
                         functools.partial(_store_heads, ow_ref.at[rs, :], g=g, rows=QSUB)))
    _attention_jobs(jobs)


def _attn_lat(sink, layer, qg, kg, vg, cgk, cgv, qw, kw, vw, cwk, cwv, seq):
    t_tokens = qg.shape[0]
    nbatch = t_tokens // seq
    qb = QSUB
    nqb = seq // qb
    past = cgk.shape[2]
    qspec = lambda wd: pl.BlockSpec((qb, wd), lambda b, i: (b * nqb + i, 0))
    kvspec = lambda wd: pl.BlockSpec((seq, wd), lambda b, i: (b, 0))
    cspec = lambda wd: pl.BlockSpec((None, None, past, wd // HEAD_DIM, HEAD_DIM), lambda b, i: (b, layer, 0, 0, 0))
    return pl.pallas_call(
        functools.partial(_attn_lat_kernel, seq=seq, qb=qb),
        grid=(nbatch, nqb),
        in_specs=[pl.BlockSpec(memory_space=pltpu.SMEM),
                  qspec(GLOB_Q), kvspec(GLOB_KVW), kvspec(GLOB_KVW), cspec(GLOB_KVW), cspec(GLOB_KVW),
                  qspec(WIN_Q), kvspec(WIN_KVW), kvspec(WIN_KVW), cspec(WIN_KVW), cspec(WIN_KVW)],
        out_specs=[qspec(GLOB_Q), qspec(WIN_Q)],
        out_shape=[jax.ShapeDtypeStruct((t_tokens, GLOB_Q), BF16),
                   jax.ShapeDtypeStruct((t_tokens, WIN_Q), BF16)],
        scratch_shapes=[pltpu.VMEM((past, wd), BF16) for wd in (GLOB_KVW, GLOB_KVW, WIN_KVW, WIN_KVW)],
        compiler_params=_params("arbitrary", "arbitrary"),
        name="attn_lat",
    )(sink, qg, kg, vg, cgk, cgv, qw, kw, vw, cwk, cwv)


def _outproj_kernel(x_ref, mod_ref, g_ref, yc_ref, yg_ref, yw_ref, w_ref, o_ref, h_ref, *, rchunk):
    c1 = CONV_CH
    c2 = CONV_CH + GLOB_Q
    for r in range(x_ref.shape[0] // rchunk):
        rs = slice(r * rchunk, (r + 1) * rchunk)
        mix = (jnp.dot(yc_ref[rs, :], w_ref[0:c1, :], preferred_element_type=F32)
               + jnp.dot(yg_ref[rs, :], w_ref[c1:c2, :], preferred_element_type=F32)
               + jnp.dot(yw_ref[rs, :], w_ref[c2:, :], preferred_element_type=F32))
        x1 = x_ref[rs, :] + mod_ref[2:3, :] * mix
        o_ref[rs, :] = x1
        h = _rms(x1, g_ref[...]) * (1.0 + mod_ref[4:5, :]) + mod_ref[3:4, :]
        h_ref[rs, :] = h.astype(BF16)


def _outproj(x, mod, row_of_tile, g_mlp, yc, yg, yw, w):
    t_tokens, d = x.shape
    tm = 512
    row = lambda wd: pl.BlockSpec((tm, wd), lambda i: (i, 0))
    return pl.pallas_call(
        functools.partial(_outproj_kernel, rchunk=256),
        grid=(t_tokens // tm,),
        in_specs=[row(d),
                  pl.BlockSpec((None, 6, d), lambda i: (row_of_tile(i, tm), 0, 0)),
                  pl.BlockSpec((1, d), lambda i: (0, 0)),
                  row(CONV_CH), row(GLOB_Q), row(WIN_Q),
                  pl.BlockSpec(w.shape, lambda i: (0, 0), pipeline_mode=pl.Buffered(1))],
        out_specs=[row(d), row(d)],
        out_shape=[jax.ShapeDtypeStruct((t_tokens, d), F32), jax.ShapeDtypeStruct((t_tokens, d), BF16)],
        compiler_params=_params("arbitrary"),
        name="outproj",
    )(x, mod, g_mlp, yc, yg, yw, w)


def _mlp_kernel(*refs, final_norm, n_cast):
    n_in = 5 + int(final_norm)
    x_ref, h_ref, mod_ref, w1_ref, w2_ref = refs[:5]
    cast_in = refs[n_in:n_in + n_cast]
    o_ref = refs[n_in + n_cast]
    cast_out = refs[n_in + n_cast + 1:]
    acc_ref = o_ref
    j = pl.program_id(1)
    last = pl.num_programs(1) - 1

    def partial_sum():
        for src, dst in zip(cast_in, cast_out):
            dst[...] = src[...].astype(BF16)
        a = jnp.maximum(jnp.dot(h_ref[...], w1_ref[...], preferred_element_type=F32), 0.0)
        return jnp.dot((a * a).astype(BF16), w2_ref[...], preferred_element_type=F32)

    @pl.when(j == 0)
    def _():
        acc_ref[...] = partial_sum()

    @pl.when(jnp.logical_and(j > 0, j < last))
    def _():
        acc_ref[...] += partial_sum()

    @pl.when(j == last)
    def _():
        y = x_ref[...] + mod_ref[5:6, :] * (acc_ref[...] + partial_sum())
        if final_norm:
            y = _rms(y, refs[5][...])
        o_ref[...] = y


def _mlp(x, h, mod, row_of_tile, w1, w2, g_final, cast_jobs):
    t_tokens, d = x.shape
    d_ff = w1.shape[1]
    tm, tf = 512, 1024
    ni, nj = t_tokens // tm, d_ff // tf
    assert nj >= 2
    final_norm = g_final is not None
    in_specs = [
        pl.BlockSpec((tm, d), lambda i, j: (i, 0)),
        pl.BlockSpec((tm, d), lambda i, j: (i, 0)),
        pl.BlockSpec((None, 6, d), lambda i, j: (row_of_tile(i, tm), 0, 0)),
        pl.BlockSpec((d, tf), lambda i, j: (0, j)),
        pl.BlockSpec((tf, d), lambda i, j: (j, 0)),
    ]
    args = [x, h, mod, w1, w2]
    if final_norm:
        in_specs.append(pl.BlockSpec((1, d), lambda i, j: (0, 0)))
        args.append(g_final)
    out_specs = [pl.BlockSpec((tm, d), lambda i, j: (i, 0))]
    out_shape = [jax.ShapeDtypeStruct((t_tokens, d), F32)]
    for w, layer in cast_jobs:
        _, rows, cols = w.shape
        rps = rows // (ni * nj)
        assert rps * ni * nj == rows and rps % 16 == 0
        in_specs.append(pl.BlockSpec((None, rps, cols), lambda i, j, layer=layer: (layer, i * nj + j, 0)))
        args.append(w)
        out_specs.append(pl.BlockSpec((rps, cols), lambda i, j: (i * nj + j, 0)))
        out_shape.append(jax.ShapeDtypeStruct((rows, cols), BF16))
    return pl.pallas_call(
        functools.partial(_mlp_kernel, final_norm=final_norm, n_cast=len(cast_jobs)),
        grid=(ni, nj),
        in_specs=in_specs,
        out_specs=out_specs,
        out_shape=out_shape,
        compiler_params=_params("arbitrary", "arbitrary"),
        name="mlp_final" if final_norm else "mlp",
    )(*args)


def _rope_tables(n_tokens):
    rows = n_tokens // GRID_W
    r, col = jnp.meshgrid(jnp.arange(rows), jnp.arange(GRID_W), indexing='ij')
    r = r.reshape(-1).astype(F32)
    col = col.reshape(-1).astype(F32)
    n_freq = HEAD_DIM // 4
    inv = ROPE_THETA ** (-jnp.arange(n_freq, dtype=F32) / n_freq)
    ang_r = r[:, None] * inv
    ang_c = col[:, None] * inv
    ang = jnp.concatenate([ang_r, ang_r, ang_c, ang_c], axis=-1)
    sign = jnp.asarray(np.tile(np.repeat(np.array([-1.0, 1.0], np.float32), n_freq), 2))
    return jnp.cos(ang), jnp.sin(ang) * sign


def kernel(x_prompt, x_sample, cache_glob_k, cache_glob_v, cache_win_k, cache_win_v, c, c_ctx, w_ada, b_ada,
           g_attn, g_mlp, w_in, conv_w, conv_b, conv_ln_g, conv_ln_b, q_norm_g, k_norm_g, sink, w_out,
           w_mlp1, w_mlp2, g_final):
    batch, seq, d = x_prompt.shape
    dec_batch, dec_seq, _ = x_sample.shape
    depth = w_ada.shape[0]
    ctx_row = dec_batch
    assert dec_batch < MOD_ROWS

    cin = jnp.concatenate([c, c_ctx[None, :], jnp.zeros((MOD_ROWS - dec_batch - 1, d), F32)], axis=0)
    mods = _ada(cin, w_ada, b_ada).reshape(depth, MOD_ROWS, 6, d)
    rope_tabs = _rope_tables(dec_seq)

    w_in_b, w_out_b, w1_b, w2_b = (w[0].astype(BF16) for w in (w_in, w_out, w_mlp1, w_mlp2))
    caches = (cache_glob_k, cache_glob_v, cache_win_k, cache_win_v)

    ctx_rows = lambda i, tm: ctx_row
    lat_rows = lambda i, tm: (i * tm) // dec_seq

    xp = x_prompt.reshape(batch * seq, d)
    xs = x_sample.reshape(dec_batch * dec_seq, d)
    new_cache = [jnp.zeros((batch, depth, seq, kv, HEAD_DIM), F32) for kv in (GLOB_KV, GLOB_KV, WIN_KV, WIN_KV)]
    for l in range(depth):
        mod = mods[l]
        row = lambda a: a[l].reshape(1, -1)
        g_fin = g_final.reshape(1, d) if l == depth - 1 else None
        conv_args = (conv_w[l], row(conv_b), row(conv_ln_g), row(conv_ln_b))
        more = l + 1 < depth
        cast_ctx = [(w_in, l + 1), (w_out, l + 1), (w_mlp1, l + 1)] if more else []
        cast_lat = [(w_mlp2, l + 1)] if more else []

        u, qg, kg, vg, qw, kw, vw, *new_cache = _inproj(
            xp, mod, ctx_rows, row(g_attn), w_in_b, l, row(q_norm_g), row(k_norm_g), None, seq, new_cache)
        yc = _conv(u, *conv_args, seq)
        yg, yw = _attn_ctx(sink[l], qg, kg, vg, qw, kw, vw, seq)
        xp, hp = _outproj(xp, mod, ctx_rows, row(g_mlp), yc, yg, yw, w_out_b)
        xp, *next_ctx = _mlp(xp, hp, mod, ctx_rows, w1_b, w2_b, g_fin, cast_ctx)

        u, qg, kg, vg, qw, kw, vw = _inproj(
            xs, mod, lat_rows, row(g_attn), w_in_b, l, row(q_norm_g), row(k_norm_g), rope_tabs, dec_seq, None)
        yc = _conv(u, *conv_args, dec_seq)
        yg, yw = _attn_lat(sink[l], l, qg, kg, vg, caches[0], caches[1], qw, kw, vw, caches[2], caches[3],
                           dec_seq)
        xs, hs = _outproj(xs, mod, lat_rows, row(g_mlp), yc, yg, yw, w_out_b)
        xs, *next_lat = _mlp(xs, hs, mod, lat_rows, w1_b, w2_b, g_fin, cast_lat)
        if more:
            (w_in_b, w_out_b, w1_b), (w2_b,) = next_ctx, next_lat

    return (xp.reshape(batch, seq, d), xs.reshape(dec_batch, dec_seq, d), *new_cache)
```

```python
import functools

import jax
import jax.numpy as jnp
import numpy as np
from jax import lax
from jax.experimental import pallas as pl
from jax.experimental.pallas import tpu as pltpu

HEAD_DIM = 128
CONV_CH = 512
CONV_WIDTH = 31
GLOB_HEADS = 6
GLOB_KV = 2
WIN_HEADS = 6
WIN_KV = 2
WINDOW = 128
GRID_W = 64
ROPE_THETA = 10000.0
NORM_EPS = 1e-6
NEG_INF = -1e30

GLOB_Q = GLOB_HEADS * HEAD_DIM
GLOB_KVW = GLOB_KV * HEAD_DIM
WIN_Q = WIN_HEADS * HEAD_DIM
WIN_KVW = WIN_KV * HEAD_DIM
GROUP = GLOB_HEADS // GLOB_KV
C_U = 0
C_QG = 2 * CONV_CH
C_KG = C_QG + GLOB_Q
C_VG = C_KG + GLOB_KVW
C_QW = C_VG + GLOB_KVW
C_KW = C_QW + WIN_Q
C_VW = C_KW + WIN_KVW
IN_WIDTH = C_VW + WIN_KVW

LOG2E = float(np.log2(np.e))
Q_PRESCALE = HEAD_DIM ** -0.5 * LOG2E

MOD_ROWS = 16
CONV_HALO = 16
SUBLANES = 8
CONV_BLOCK = 256
QSUB = 128
V7X_VMEM_LIMIT = 56 * 1024 * 1024

BF16 = jnp.bfloat16
F32 = jnp.float32


def _params(*sem):
    return pltpu.CompilerParams(dimension_semantics=sem, vmem_limit_bytes=V7X_VMEM_LIMIT)


def _sigmoid(x):
    return 1.0 / (1.0 + jnp.exp(-x))


def _rms(x, gain):
    return x * lax.rsqrt(jnp.mean(x * x, axis=-1, keepdims=True) + NORM_EPS) * gain


def _ada_kernel(c_ref, w_ref, b_ref, o_ref):
    c = c_ref[...]
    a = (c * _sigmoid(c)).astype(BF16)
    o_ref[...] = jnp.dot(a, w_ref[...].astype(BF16), preferred_element_type=F32) + b_ref[...]


def _ada(cin, w_ada, b_ada):
    depth, d, n = w_ada.shape
    tn = 1024
    return pl.pallas_call(
        _ada_kernel,
        grid=(depth, n // tn),
        in_specs=[
            pl.BlockSpec((MOD_ROWS, d), lambda l, j: (0, 0)),
            pl.BlockSpec((None, d, tn), lambda l, j: (l, 0, j)),
            pl.BlockSpec((None, 1, tn), lambda l, j: (l, 0, j)),
        ],
        out_specs=pl.BlockSpec((None, MOD_ROWS, tn), lambda l, j: (l, 0, j)),
        out_shape=jax.ShapeDtypeStruct((depth, MOD_ROWS, n), F32),
        compiler_params=_params("arbitrary", "arbitrary"),
        name="ada_mod",
    )(cin, w_ada, b_ada.reshape(depth, 1, n))


def _store_cache(ref, kv, cols, t):
    nb, rows, _ = ref.shape
    seq = rows // kv
    heads = range(kv)[slice(cols.start and cols.start // HEAD_DIM, cols.stop and cols.stop // HEAD_DIM)]
    for b in range(nb):
        for i, h in enumerate(heads):
            ref[b, pl.ds(h, seq, stride=kv), :] = t[b * seq:(b + 1) * seq, i * HEAD_DIM:(i + 1) * HEAD_DIM]


def _project(x_ref, mod_ref, g_ref, w_ref, qn_ref, kn_ref, rope_refs, outs, caches):
    u_ref, qg_ref, kg_ref, vg_ref, qw_ref, kw_ref, vw_ref = outs
    h = _rms(x_ref[...], g_ref[...]) * (1.0 + mod_ref[1:2, :]) + mod_ref[0:1, :]
    hb = h.astype(BF16)

    def proj(c0, c1):
        return jnp.dot(hb, w_ref[:, c0:c1], preferred_element_type=F32)

    if rope_refs is not None:
        cos = rope_refs[0][...]
        sin = rope_refs[1][...]
        lane = lax.broadcasted_iota(jnp.int32, cos.shape, 1)
        take_next = ((lane // (HEAD_DIM // 4)) % 2) == 0

        def rot(t):
            r = jnp.where(take_next, pltpu.roll(t, HEAD_DIM - HEAD_DIM // 4, 1),
                          pltpu.roll(t, HEAD_DIM // 4, 1))
            return t * cos + r * sin
    else:
        def rot(t):
            return t

    def heads(p, n):
        return [(slice(i * HEAD_DIM, (i + 1) * HEAD_DIM), p[:, i * HEAD_DIM:(i + 1) * HEAD_DIM])
                for i in range(n)]

    u_ref[...] = proj(C_U, C_QG)

    qn = qn_ref[...]
    kn = kn_ref[...]
    for cols, t in heads(proj(C_QG, C_KG), GLOB_HEADS):
        qg_ref[:, cols] = (rot(_rms(t, qn)) * Q_PRESCALE).astype(BF16)
    for cols, t in heads(proj(C_KG, C_VG), GLOB_KV):
        t = _rms(t, kn)
        if caches is not None:
            _store_cache(caches[0], GLOB_KV, cols, t)
        kg_ref[:, cols] = rot(t).astype(BF16)
    p = proj(C_VG, C_QW)
    if caches is not None:
        _store_cache(caches[1], GLOB_KV, slice(None), p)
    vg_ref[...] = p.astype(BF16)
    for cols, t in heads(proj(C_QW, C_KW), WIN_HEADS):
        qw_ref[:, cols] = (rot(t) * Q_PRESCALE).astype(BF16)
    p = proj(C_KW, C_VW)
    if caches is not None:
        _store_cache(caches[2], WIN_KV, slice(None), p)
    for cols, t in heads(p, WIN_KV):
        kw_ref[:, cols] = rot(t).astype(BF16)
    p = proj(C_VW, IN_WIDTH)
    if caches is not None:
        _store_cache(caches[3], WIN_KV, slice(None), p)
    vw_ref[...] = p.astype(BF16)


PROJ_WIDTHS = ((C_QG - C_U, F32), (GLOB_Q, BF16), (GLOB_KVW, BF16), (GLOB_KVW, BF16),
               (WIN_Q, BF16), (WIN_KVW, BF16), (WIN_KVW, BF16))


def _inproj_kernel(*refs, rope, n_aliased):
    x_ref, mod_ref, g_ref, w_ref, qn_ref, kn_ref = refs[:6]
    pos = 6
    rope_refs = None
    if rope:
        rope_refs = refs[pos:pos + 2]
        pos += 2
    pos += n_aliased
    outs = refs[pos:pos + len(PROJ_WIDTHS)]
    caches = refs[pos + len(PROJ_WIDTHS):] or None
    _project(x_ref, mod_ref, g_ref, w_ref, qn_ref, kn_ref, rope_refs, outs, caches)


def _inproj(x, mod, row_of_tile, g, w, layer, qn, kn, rope_tabs, seq, prev_cache):
    t_tokens, d = x.shape
    tm = 512
    rope = rope_tabs is not None
    const = lambda i: (0, 0)
    in_specs = [
        pl.BlockSpec((tm, d), lambda i: (i, 0)),
        pl.BlockSpec((None, 6, d), lambda i: (row_of_tile(i, tm), 0, 0)),
        pl.BlockSpec((1, d), const),
        pl.BlockSpec((d, IN_WIDTH), const, pipeline_mode=pl.Buffered(1)),
        pl.BlockSpec((1, HEAD_DIM), const),
        pl.BlockSpec((1, HEAD_DIM), const),
    ]
    args = [x, mod, g, w, qn, kn]
    if rope:
        per_seq = seq // tm
        in_specs += [pl.BlockSpec((tm, HEAD_DIM), lambda i: (i % per_seq, 0))] * 2
        args += list(rope_tabs)
    out_specs = [pl.BlockSpec((tm, wd), lambda i: (i, 0)) for wd, _ in PROJ_WIDTHS]
    out_shape = [jax.ShapeDtypeStruct((t_tokens, wd), dt) for wd, dt in PROJ_WIDTHS]
    aliases = {}
    n_aliased = 0
    if prev_cache is not None:
        nb = tm // seq
        n_aliased = len(prev_cache)
        for prev in prev_cache:
            aliases[len(args)] = len(out_shape)
            in_specs.append(pl.BlockSpec(memory_space=pl.ANY))
            args.append(prev)
            out_specs.append(pl.BlockSpec((nb, None) + prev.shape[2:], lambda i: (i, layer, 0, 0)))
            out_shape.append(jax.ShapeDtypeStruct(prev.shape, prev.dtype))
    return pl.pallas_call(
        functools.partial(_inproj_kernel, rope=rope, n_aliased=n_aliased),
        grid=(t_tokens // tm,),
        in_specs=in_specs,
        out_specs=out_specs,
        out_shape=out_shape,
        input_output_aliases=aliases,
        compiler_params=_params("arbitrary"),
        name="inproj_rope" if rope else "inproj_ctx",
    )(*args)


def _conv_body(u_ref, halo, w_ref, b_ref, lg_ref, lb_ref, o_ref, hp_ref, y_ref):
    rows = u_ref.shape[0]

    def glu(u):
        return u[:, :CONV_CH] * _sigmoid(u[:, CONV_CH:])

    zeros = jnp.zeros((CONV_HALO, CONV_CH), F32)
    if halo is not None:
        up_ref, un_ref, has_prev, has_next = halo
        hp_ref[0:CONV_HALO, :] = jnp.where(has_prev, glu(up_ref[...]), zeros)
        hp_ref[CONV_HALO + rows:, :] = jnp.where(has_next, glu(un_ref[...]), zeros)
    else:
        hp_ref[0:CONV_HALO, :] = zeros
        hp_ref[CONV_HALO + rows:, :] = zeros
    hp_ref[CONV_HALO:CONV_HALO + rows, :] = glu(u_ref[...])

    rchunk = 128
    base = CONV_HALO - CONV_WIDTH // 2
    groups = -(-(base + CONV_WIDTH) // SUBLANES)
    for c in range(CONV_CH // HEAD_DIM):
        cs = slice(c * HEAD_DIM, (c + 1) * HEAD_DIM)
        for r in range(rows // rchunk):
            acc = None
            for s in range(SUBLANES):
                part = None
                for q in range(groups):
                    k = q * SUBLANES + s - base
                    if 0 <= k < CONV_WIDTH:
                        r0 = r * rchunk + q * SUBLANES
                        term = hp_ref[r0:r0 + rchunk + SUBLANES, cs] * w_ref[k:k + 1, cs]
                        part = term if part is None else part + term
                part = part[s:s + rchunk]
                acc = part if acc is None else acc + part
            y_ref[r * rchunk:(r + 1) * rchunk, cs] = acc

    y = y_ref[...] + b_ref[...]
    mu = jnp.mean(y, axis=-1, keepdims=True)
    yc = y - mu
    var = jnp.mean(yc * yc, axis=-1, keepdims=True)
    z = yc * lax.rsqrt(var + NORM_EPS) * lg_ref[...] + lb_ref[...]
    o_ref[...] = (z * _sigmoid(z)).astype(BF16)


def _conv_kernel(*refs, blocks_per_seq):
    if blocks_per_seq > 1:
        u_ref, up_ref, un_ref = refs[:3]
        j = pl.program_id(0) % blocks_per_seq
        halo = (up_ref, un_ref, j > 0, j < blocks_per_seq - 1)
        rest = refs[3:]
    else:
        u_ref, halo, rest = refs[0], None, refs[1:]
    _conv_body(u_ref, halo, *rest)


def _conv(u, w, b, lg, lb, seq):
    t_tokens = u.shape[0]
    rows = CONV_BLOCK
    blocks_per_seq = seq // rows
    halo_per_block = rows // CONV_HALO
    n_halo = t_tokens // CONV_HALO
    const = lambda i: (0, 0)
    in_specs = [pl.BlockSpec((rows, 2 * CONV_CH), lambda i: (i, 0))]
    args = [u]
    if blocks_per_seq > 1:
        in_specs += [
            pl.BlockSpec((CONV_HALO, 2 * CONV_CH), lambda i: (jnp.maximum(i * halo_per_block - 1, 0), 0)),
            pl.BlockSpec((CONV_HALO, 2 * CONV_CH),
                         lambda i: (jnp.minimum((i + 1) * halo_per_block, n_halo - 1), 0)),
        ]
        args += [u, u]
    in_specs += [pl.BlockSpec((CONV_WIDTH, CONV_CH), const)] + [pl.BlockSpec((1, CONV_CH), const)] * 3
    args += [w, b, lg, lb]
    return pl.pallas_call(
        functools.partial(_conv_kernel, blocks_per_seq=blocks_per_seq),
        grid=(t_tokens // rows,),
        in_specs=in_specs,
        out_specs=pl.BlockSpec((rows, CONV_CH), lambda i: (i, 0)),
        out_shape=jax.ShapeDtypeStruct((t_tokens, CONV_CH), BF16),
        scratch_shapes=[pltpu.VMEM((rows + 2 * CONV_HALO, CONV_CH), F32),
                        pltpu.VMEM((rows, CONV_CH), F32)],
        compiler_params=_params("arbitrary"),
        name="conv_mixer",
    )(*args)


def _stack_heads(q, g):
    return jnp.concatenate(
        [q[:, (g * GROUP + j) * HEAD_DIM:(g * GROUP + j + 1) * HEAD_DIM] for j in range(GROUP)], axis=0)


def _scores(q, k):
    return lax.dot_general(q, k, (((1,), (1,)), ((), ())), preferred_element_type=F32)


def _sink_column(sink_ref, g, rows):
    return jnp.concatenate(
        [jnp.full((rows, 1), sink_ref[g * GROUP + j] * LOG2E, F32) for j in range(GROUP)], axis=0)


def _store_heads(o_ref, o, g, rows):
    for j in range(GROUP):
        c0 = (g * GROUP + j) * HEAD_DIM
        o_ref[:, c0:c0 + HEAD_DIM] = o[j * rows:(j + 1) * rows].astype(o_ref.dtype)


def _attention_jobs(jobs):
    n = len(jobs)
    pieces = [None] * n
    soft = [None] * n
    for t in range(n + 2):
        if t < n:
            pieces[t] = jobs[t][0]()
        if 0 <= t - 2 < n:
            es, rden = soft[t - 2]
            o = functools.reduce(jnp.add, [jnp.dot(e, v, preferred_element_type=F32)
                                           for e, (_, v) in zip(es, pieces[t - 2])])
            jobs[t - 2][2](o * rden)
        if 0 <= t - 1 < n:
            snk = jobs[t - 1][1]
            m = functools.reduce(jnp.maximum, [jnp.max(s, axis=-1, keepdims=True) for s, _ in pieces[t - 1]])
            if snk is not None:
                m = jnp.maximum(m, snk)
            es = [jnp.exp2(s - m) for s, _ in pieces[t - 1]]
            den = functools.reduce(jnp.add, [jnp.sum(e, axis=-1, keepdims=True) for e in es])
            if snk is not None:
                den = den + jnp.exp2(snk - m)
            soft[t - 1] = ([e.astype(BF16) for e in es], 1.0 / den)


def _attn_ctx_kernel(sink_ref, qg_ref, kg_ref, vg_ref, qw_ref, kw_ref, vw_ref, og_ref, ow_ref, *, seq):
    jobs = []
    for b in range(qg_ref.shape[0] // seq):
        rs = slice(b * seq, (b + 1) * seq)
        for g in range(GLOB_KV):
            hs = slice(g * HEAD_DIM, (g + 1) * HEAD_DIM)

            def glob_scores(rs=rs, hs=hs, g=g):
                return [(_scores(_stack_heads(qg_ref[rs, :], g), kg_ref[rs, hs]), vg_ref[rs, hs])]

            def win_scores(rs=rs, hs=hs, g=g):
                return [(_scores(_stack_heads(qw_ref[rs, :], g), kw_ref[rs, hs]), vw_ref[rs, hs])]

            jobs.append((glob_scores, None, functools.partial(_store_heads, og_ref.at[rs, :], g=g, rows=seq)))
            jobs.append((win_scores, _sink_column(sink_ref, g, seq),
                         functools.partial(_store_heads, ow_ref.at[rs, :], g=g, rows=seq)))
    _attention_jobs(jobs)


def _attn_ctx(sink, qg, kg, vg, qw, kw, vw, seq):
    t_tokens = qg.shape[0]
    rows = 2 * seq
    spec = lambda wd: pl.BlockSpec((rows, wd), lambda i: (i, 0))
    return pl.pallas_call(
        functools.partial(_attn_ctx_kernel, seq=seq),
        grid=(t_tokens // rows,),
        in_specs=[pl.BlockSpec(memory_space=pltpu.SMEM),
                  spec(GLOB_Q), spec(GLOB_KVW), spec(GLOB_KVW), spec(WIN_Q), spec(WIN_KVW), spec(WIN_KVW)],
        out_specs=[spec(GLOB_Q), spec(WIN_Q)],
        out_shape=[jax.ShapeDtypeStruct((t_tokens, GLOB_Q), BF16),
                   jax.ShapeDtypeStruct((t_tokens, WIN_Q), BF16)],
        compiler_params=_params("arbitrary"),
        name="attn_ctx",
    )(sink, qg, kg, vg, qw, kw, vw)


def _attn_lat_kernel(sink_ref, qg_ref, kg_ref, vg_ref, cgk_ref, cgv_ref,
                     qw_ref, kw_ref, vw_ref, cwk_ref, cwv_ref, og_ref, ow_ref,
                     cgk_b, cgv_b, cwk_b, cwv_b, *, seq, qb):
    bi = pl.program_id(1)

    @pl.when(bi == 0)
    def _():
        for src, dst in ((cgk_ref, cgk_b), (cgv_ref, cgv_b), (cwk_ref, cwk_b), (cwv_ref, cwv_b)):
            past = dst.shape[0]
            kv = src.shape[0] // past
            for h in range(kv):
                dst[:, h * HEAD_DIM:(h + 1) * HEAD_DIM] = src[pl.ds(h, past, stride=kv), :].astype(BF16)

    span = QSUB + 2 * WINDOW
    jobs = []
    for sb in range(qb // QSUB):
        rs = slice(sb * QSUB, (sb + 1) * QSUB)
        q0 = bi * qb + sb * QSUB
        start = pl.multiple_of(jnp.clip(q0 - WINDOW, 0, seq - span), HEAD_DIM)
        qpos = q0 + lax.broadcasted_iota(jnp.int32, (QSUB, span), 0)
        kpos = start + lax.broadcasted_iota(jnp.int32, (QSUB, span), 1)
        band = jnp.where(jnp.abs(qpos - kpos) <= WINDOW, 0.0, NEG_INF)
        band = jnp.concatenate([band] * GROUP, axis=0)
        for g in range(GLOB_KV):
            hs = slice(g * HEAD_DIM, (g + 1) * HEAD_DIM)

            def glob_scores(rs=rs, hs=hs, g=g):
                q = _stack_heads(qg_ref[rs, :], g)
                return [(_scores(q, cgk_b[:, hs]), cgv_b[:, hs]), (_scores(q, kg_ref[:, hs]), vg_ref[:, hs])]

            def win_scores(rs=rs, hs=hs, g=g, start=start, band=band):
                q = _stack_heads(qw_ref[rs, :], g)
                s_loc = _scores(q, kw_ref[pl.ds(start, span), hs]) + band
                return [(_scores(q, cwk_b[:, hs]), cwv_b[:, hs]), (s_loc, vw_ref[pl.ds(start, span), hs])]

            jobs.append((glob_scores, None, functools.partial(_store_heads, og_ref.at[rs, :], g=g, rows=QSUB)))
            jobs.append((win_scores, _sink_column(sink_ref, g, QSUB),
                         functools.partial(_store_heads, ow_ref.at[rs, :], g=g, rows=QSUB)))
    _attention_jobs(jobs)


def _attn_lat(sink, layer, qg, kg, vg, cgk, cgv, qw, kw, vw, cwk, cwv, seq):
    t_tokens = qg.shape[0]
    nbatch = t_tokens // seq
    qb = QSUB
    nqb = seq // qb
    past = cgk.shape[2] // GLOB_KV
    qspec = lambda wd: pl.BlockSpec((qb, wd), lambda b, i: (b * nqb + i, 0))
    kvspec = lambda wd: pl.BlockSpec((seq, wd), lambda b, i: (b, 0))
    cspec = lambda wd: pl.BlockSpec((None, None, past * (wd // HEAD_DIM), HEAD_DIM), lambda b, i: (b, layer, 0, 0))
    return pl.pallas_call(
        functools.partial(_attn_lat_kernel, seq=seq, qb=qb),
        grid=(nbatch, nqb),
        in_specs=[pl.BlockSpec(memory_space=pltpu.SMEM),
                  qspec(GLOB_Q), kvspec(GLOB_KVW), kvspec(GLOB_KVW), cspec(GLOB_KVW), cspec(GLOB_KVW),
                  qspec(WIN_Q), kvspec(WIN_KVW), kvspec(WIN_KVW), cspec(WIN_KVW), cspec(WIN_KVW)],
        out_specs=[qspec(GLOB_Q), qspec(WIN_Q)],
        out_shape=[jax.ShapeDtypeStruct((t_tokens, GLOB_Q), BF16),
                   jax.ShapeDtypeStruct((t_tokens, WIN_Q), BF16)],
        scratch_shapes=[pltpu.VMEM((past, wd), BF16) for wd in (GLOB_KVW, GLOB_KVW, WIN_KVW, WIN_KVW)],
        compiler_params=_params("arbitrary", "arbitrary"),
        name="attn_lat",
    )(sink, qg, kg, vg, cgk, cgv, qw, kw, vw, cwk, cwv)


def _outproj_kernel(x_ref, mod_ref, g_ref, yc_ref, yg_ref, yw_ref, w_ref, o_ref, h_ref, *, rchunk):
    c1 = CONV_CH
    c2 = CONV_CH + GLOB_Q
    for r in range(x_ref.shape[0] // rchunk):
        rs = slice(r * rchunk, (r + 1) * rchunk)
        mix = (jnp.dot(yc_ref[rs, :], w_ref[0:c1, :], preferred_element_type=F32)
               + jnp.dot(yg_ref[rs, :], w_ref[c1:c2, :], preferred_element_type=F32)
               + jnp.dot(yw_ref[rs, :], w_ref[c2:, :], preferred_element_type=F32))
        x1 = x_ref[rs, :] + mod_ref[2:3, :] * mix
        o_ref[rs, :] = x1
        h = _rms(x1, g_ref[...]) * (1.0 + mod_ref[4:5, :]) + mod_ref[3:4, :]
        h_ref[rs, :] = h.astype(BF16)


def _outproj(x, mod, row_of_tile, g_mlp, yc, yg, yw, w):
    t_tokens, d = x.shape
    tm = 512
    row = lambda wd: pl.BlockSpec((tm, wd), lambda i: (i, 0))
    return pl.pallas_call(
        functools.partial(_outproj_kernel, rchunk=256),
        grid=(t_tokens // tm,),
        in_specs=[row(d),
                  pl.BlockSpec((None, 6, d), lambda i: (row_of_tile(i, tm), 0, 0)),
                  pl.BlockSpec((1, d), lambda i: (0, 0)),
                  row(CONV_CH), row(GLOB_Q), row(WIN_Q),
                  pl.BlockSpec(w.shape, lambda i: (0, 0), pipeline_mode=pl.Buffered(1))],
        out_specs=[row(d), row(d)],
        out_shape=[jax.ShapeDtypeStruct((t_tokens, d), F32), jax.ShapeDtypeStruct((t_tokens, d), BF16)],
        compiler_params=_params("arbitrary"),
        name="outproj",
    )(x, mod, g_mlp, yc, yg, yw, w)


def _mlp_kernel(*refs, final_norm, n_cast):
    n_in = 5 + int(final_norm)
    x_ref, h_ref, mod_ref, w1_ref, w2_ref = refs[:5]
    cast_in = refs[n_in:n_in + n_cast]
    o_ref = refs[n_in + n_cast]
    cast_out = refs[n_in + n_cast + 1:]
    acc_ref = o_ref
    j = pl.program_id(1)
    last = pl.num_programs(1) - 1

    def partial_sum():
        for src, dst in zip(cast_in, cast_out):
            dst[...] = src[...].astype(BF16)
        a = jnp.maximum(jnp.dot(h_ref[...], w1_ref[...], preferred_element_type=F32), 0.0)
        return jnp.dot((a * a).astype(BF16), w2_ref[...], preferred_element_type=F32)

    @pl.when(j == 0)
    def _():
        acc_ref[...] = partial_sum()

    @pl.when(jnp.logical_and(j > 0, j < last))
    def _():
        acc_ref[...] += partial_sum()

    @pl.when(j == last)
    def _():
        y = x_ref[...] + mod_ref[5:6, :] * (acc_ref[...] + partial_sum())
        if final_norm:
            y = _rms(y, refs[5][...])
        o_ref[...] = y


def _mlp(x, h, mod, row_of_tile, w1, w2, g_final, cast_jobs):
    t_tokens, d = x.shape
    d_ff = w1.shape[1]
    tm, tf = 512, 1024
    ni, nj = t_tokens // tm, d_ff // tf
    assert nj >= 2
    final_norm = g_final is not None
    in_specs = [
        pl.BlockSpec((tm, d), lambda i, j: (i, 0)),
        pl.BlockSpec((tm, d), lambda i, j: (i, 0)),
        pl.BlockSpec((None, 6, d), lambda i, j: (row_of_tile(i, tm), 0, 0)),
        pl.BlockSpec((d, tf), lambda i, j: (0, j)),
        pl.BlockSpec((tf, d), lambda i, j: (j, 0)),
    ]
    args = [x, h, mod, w1, w2]
    if final_norm:
        in_specs.append(pl.BlockSpec((1, d), lambda i, j: (0, 0)))
        args.append(g_final)
    out_specs = [pl.BlockSpec((tm, d), lambda i, j: (i, 0))]
    out_shape = [jax.ShapeDtypeStruct((t_tokens, d), F32)]
    for w, layer in cast_jobs:
        _, rows, cols = w.shape
        rps = rows // (ni * nj)
        assert rps * ni * nj == rows and rps % 16 == 0
        in_specs.append(pl.BlockSpec((None, rps, cols), lambda i, j, layer=layer: (layer, i * nj + j, 0)))
        args.append(w)
        out_specs.append(pl.BlockSpec((rps, cols), lambda i, j: (i * nj + j, 0)))
        out_shape.append(jax.ShapeDtypeStruct((rows, cols), BF16))
    return pl.pallas_call(
        functools.partial(_mlp_kernel, final_norm=final_norm, n_cast=len(cast_jobs)),
        grid=(ni, nj),
        in_specs=in_specs,
        out_specs=out_specs,
        out_shape=out_shape,
        compiler_params=_params("arbitrary", "arbitrary"),
        name="mlp_final" if final_norm else "mlp",
    )(*args)


def _rope_tables(n_tokens):
    rows = n_tokens // GRID_W
    r, col = jnp.meshgrid(jnp.arange(rows), jnp.arange(GRID_W), indexing='ij')
    r = r.reshape(-1).astype(F32)
    col = col.reshape(-1).astype(F32)
    n_freq = HEAD_DIM // 4
    inv = ROPE_THETA ** (-jnp.arange(n_freq, dtype=F32) / n_freq)
    ang_r = r[:, None] * inv
    ang_c = col[:, None] * inv
    ang = jnp.concatenate([ang_r, ang_r, ang_c, ang_c], axis=-1)
    sign = jnp.asarray(np.tile(np.repeat(np.array([-1.0, 1.0], np.float32), n_freq), 2))
    return jnp.cos(ang), jnp.sin(ang) * sign


def kernel(x_prompt, x_sample, cache_glob_k, cache_glob_v, cache_win_k, cache_win_v, c, c_ctx, w_ada, b_ada,
           g_attn, g_mlp, w_in, conv_w, conv_b, conv_ln_g, conv_ln_b, q_norm_g, k_norm_g, sink, w_out,
           w_mlp1, w_mlp2, g_final):
    batch, seq, d = x_prompt.shape
    dec_batch, dec_seq, _ = x_sample.shape
    depth = w_ada.shape[0]
    ctx_row = dec_batch
    assert dec_batch < MOD_ROWS

    cin = jnp.concatenate([c, c_ctx[None, :], jnp.zeros((MOD_ROWS - dec_batch - 1, d), F32)], axis=0)
    mods = _ada(cin, w_ada, b_ada).reshape(depth, MOD_ROWS, 6, d)
    rope_tabs = _rope_tables(dec_seq)

    w_in_b, w_out_b, w1_b, w2_b = (w[0].astype(BF16) for w in (w_in, w_out, w_mlp1, w_mlp2))
    caches = [a.reshape(dec_batch, depth, -1, HEAD_DIM) for a in (cache_glob_k, cache_glob_v, cache_win_k, cache_win_v)]

    ctx_rows = lambda i, tm: ctx_row
    lat_rows = lambda i, tm: (i * tm) // dec_seq

    xp = x_prompt.reshape(batch * seq, d)
    xs = x_sample.reshape(dec_batch * dec_seq, d)
    new_cache = [jnp.zeros((batch, depth, seq * kv, HEAD_DIM), F32) for kv in (GLOB_KV, GLOB_KV, WIN_KV, WIN_KV)]
    for l in range(depth):
        mod = mods[l]
        row = lambda a: a[l].reshape(1, -1)
        g_fin = g_final.reshape(1, d) if l == depth - 1 else None
        conv_args = (conv_w[l], row(conv_b), row(conv_ln_g), row(conv_ln_b))
        more = l + 1 < depth
        cast_ctx = [(w_in, l + 1), (w_out, l + 1), (w_mlp1, l + 1)] if more else []
        cast_lat = [(w_mlp2, l + 1)] if more else []

        u, qg, kg, vg, qw, kw, vw, *new_cache = _inproj(
            xp, mod, ctx_rows, row(g_attn), w_in_b, l, row(q_norm_g), row(k_norm_g), None, seq, new_cache)
        yc = _conv(u, *conv_args, seq)
        yg, yw = _attn_ctx(sink[l], qg, kg, vg, qw, kw, vw, seq)
        xp, hp = _outproj(xp, mod, ctx_rows, row(g_mlp), yc, yg, yw, w_out_b)
        xp, *next_ctx = _mlp(xp, hp, mod, ctx_rows, w1_b, w2_b, g_fin, cast_ctx)

        u, qg, kg, vg, qw, kw, vw = _inproj(
            xs, mod, lat_rows, row(g_attn), w_in_b, l, row(q_norm_g), row(k_norm_g), rope_tabs, dec_seq, None)
        yc = _conv(u, *conv_args, dec_seq)
        yg, yw = _attn_lat(sink[l], l, qg, kg, vg, caches[0], caches[1], qw, kw, vw, caches[2], caches[3],
                           dec_seq)
        xs, hs = _outproj(xs, mod, lat_rows, row(g_mlp), yc, yg, yw, w_out_b)
        xs, *next_lat = _mlp(xs, hs, mod, lat_rows, w1_b, w2_b, g_fin, cast_lat)
        if more:
            (w_in_b, w_out_b, w1_b), (w2_b,) = next_ctx, next_lat

    outs = [a.reshape(batch, depth, seq, -1, HEAD_DIM) for a in new_cache]
    return (xp.reshape(batch, seq, d), xs.reshape(dec_batch, dec_seq, d), *outs)
```

```python
import functools

import jax
import jax.numpy as jnp
import numpy as np
from jax import lax
from jax.experimental import pallas as pl
from jax.experimental.pallas import tpu as pltpu

HEAD_DIM = 128
CONV_CH = 512
CONV_WIDTH = 31
GLOB_HEADS = 6
GLOB_KV = 2
WIN_HEADS = 6
WIN_KV = 2
WINDOW = 128
GRID_W = 64
ROPE_THETA = 10000.0
NORM_EPS = 1e-6
NEG_INF = -1e30

GLOB_Q = GLOB_HEADS * HEAD_DIM
GLOB_KVW = GLOB_KV * HEAD_DIM
WIN_Q = WIN_HEADS * HEAD_DIM
WIN_KVW = WIN_KV * HEAD_DIM
GROUP = GLOB_HEADS // GLOB_KV
C_U = 0
C_QG = 2 * CONV_CH
C_KG = C_QG + GLOB_Q
C_VG = C_KG + GLOB_KVW
C_QW = C_VG + GLOB_KVW
C_KW = C_QW + WIN_Q
C_VW = C_KW + WIN_KVW
IN_WIDTH = C_VW + WIN_KVW

LOG2E = float(np.log2(np.e))
Q_PRESCALE = HEAD_DIM ** -0.5 * LOG2E

MOD_ROWS = 16
CONV_HALO = 16
SUBLANES = 8
CONV_BLOCK = 256
QSUB = 128
SCORE_LOOKAHEAD = 1
V7X_VMEM_LIMIT = 56 * 1024 * 1024

BF16 = jnp.bfloat16
F32 = jnp.float32


def _params(*sem):
    return pltpu.CompilerParams(dimension_semantics=sem, vmem_limit_bytes=V7X_VMEM_LIMIT)


def _sigmoid(x):
    return 1.0 / (1.0 + jnp.exp(-x))


def _rms(x, gain):
    return x * lax.rsqrt(jnp.mean(x * x, axis=-1, keepdims=True) + NORM_EPS) * gain


def _ada_kernel(c_ref, w_ref, b_ref, o_ref):
    c = c_ref[...]
    a = (c * _sigmoid(c)).astype(BF16)
    o_ref[...] = jnp.dot(a, w_ref[...].astype(BF16), preferred_element_type=F32) + b_ref[...]


def _ada(cin, w_ada, b_ada):
    depth, d, n = w_ada.shape
    tn = 1024
    return pl.pallas_call(
        _ada_kernel,
        grid=(depth, n // tn),
        in_specs=[
            pl.BlockSpec((MOD_ROWS, d), lambda l, j: (0, 0)),
            pl.BlockSpec((None, d, tn), lambda l, j: (l, 0, j)),
            pl.BlockSpec((None, 1, tn), lambda l, j: (l, 0, j)),
        ],
        out_specs=pl.BlockSpec((None, MOD_ROWS, tn), lambda l, j: (l, 0, j)),
        out_shape=jax.ShapeDtypeStruct((depth, MOD_ROWS, n), F32),
        compiler_params=_params("arbitrary", "arbitrary"),
        name="ada_mod",
    )(cin, w_ada, b_ada.reshape(depth, 1, n))


def _store_cache(ref, kv, cols, t):
    nb, rows, _ = ref.shape
    seq = rows // kv
    heads = range(kv)[slice(cols.start and cols.start // HEAD_DIM, cols.stop and cols.stop // HEAD_DIM)]
    for b in range(nb):
        for i, h in enumerate(heads):
            ref[b, pl.ds(h, seq, stride=kv), :] = t[b * seq:(b + 1) * seq, i * HEAD_DIM:(i + 1) * HEAD_DIM]


def _project(x_ref, mod_ref, g_ref, w_ref, qn_ref, kn_ref, rope_refs, outs, caches):
    u_ref, qg_ref, kg_ref, vg_ref, qw_ref, kw_ref, vw_ref = outs
    h = _rms(x_ref[...], g_ref[...]) * (1.0 + mod_ref[1:2, :]) + mod_ref[0:1, :]
    hb = h.astype(BF16)

    def proj(c0, c1):
        return jnp.dot(hb, w_ref[:, c0:c1], preferred_element_type=F32)

    if rope_refs is not None:
        cos = rope_refs[0][...]
        sin = rope_refs[1][...]
        lane = lax.broadcasted_iota(jnp.int32, cos.shape, 1)
        take_next = ((lane // (HEAD_DIM // 4)) % 2) == 0

        def rot(t):
            r = jnp.where(take_next, pltpu.roll(t, HEAD_DIM - HEAD_DIM // 4, 1),
                          pltpu.roll(t, HEAD_DIM // 4, 1))
            return t * cos + r * sin
    else:
        def rot(t):
            return t

    def heads(p, n):
        return [(slice(i * HEAD_DIM, (i + 1) * HEAD_DIM), p[:, i * HEAD_DIM:(i + 1) * HEAD_DIM])
                for i in range(n)]

    u_ref[...] = proj(C_U, C_QG)

    qn = qn_ref[...]
    kn = kn_ref[...]
    for cols, t in heads(proj(C_QG, C_KG), GLOB_HEADS):
        qg_ref[:, cols] = (rot(_rms(t, qn)) * Q_PRESCALE).astype(BF16)
    for cols, t in heads(proj(C_KG, C_VG), GLOB_KV):
        t = _rms(t, kn)
        if caches is not None:
            _store_cache(caches[0], GLOB_KV, cols, t)
        kg_ref[:, cols] = rot(t).astype(BF16)
    p = proj(C_VG, C_QW)
    if caches is not None:
        _store_cache(caches[1], GLOB_KV, slice(None), p)
    vg_ref[...] = p.astype(BF16)
    for cols, t in heads(proj(C_QW, C_KW), WIN_HEADS):
        qw_ref[:, cols] = (rot(t) * Q_PRESCALE).astype(BF16)
    p = proj(C_KW, C_VW)
    if caches is not None:
        _store_cache(caches[2], WIN_KV, slice(None), p)
    for cols, t in heads(p, WIN_KV):
        kw_ref[:, cols] = rot(t).astype(BF16)
    p = proj(C_VW, IN_WIDTH)
    if caches is not None:
        _store_cache(caches[3], WIN_KV, slice(None), p)
    vw_ref[...] = p.astype(BF16)


PROJ_WIDTHS = ((C_QG - C_U, F32), (GLOB_Q, BF16), (GLOB_KVW, BF16), (GLOB_KVW, BF16),
               (WIN_Q, BF16), (WIN_KVW, BF16), (WIN_KVW, BF16))


def _inproj_kernel(*refs, rope, n_aliased):
    x_ref, mod_ref, g_ref, w_ref, qn_ref, kn_ref = refs[:6]
    pos = 6
    rope_refs = None
    if rope:
        rope_refs = refs[pos:pos + 2]
        pos += 2
    pos += n_aliased
    outs = refs[pos:pos + len(PROJ_WIDTHS)]
    caches = refs[pos + len(PROJ_WIDTHS):] or None
    _project(x_ref, mod_ref, g_ref, w_ref, qn_ref, kn_ref, rope_refs, outs, caches)


def _inproj(x, mod, row_of_tile, g, w, layer, qn, kn, rope_tabs, seq, prev_cache):
    t_tokens, d = x.shape
    tm = 512
    rope = rope_tabs is not None
    const = lambda i: (0, 0)
    in_specs = [
        pl.BlockSpec((tm, d), lambda i: (i, 0)),
        pl.BlockSpec((None, 6, d), lambda i: (row_of_tile(i, tm), 0, 0)),
        pl.BlockSpec((1, d), const),
        pl.BlockSpec((d, IN_WIDTH), const, pipeline_mode=pl.Buffered(1)),
        pl.BlockSpec((1, HEAD_DIM), const),
        pl.BlockSpec((1, HEAD_DIM), const),
    ]
    args = [x, mod, g, w, qn, kn]
    if rope:
        per_seq = seq // tm
        in_specs += [pl.BlockSpec((tm, HEAD_DIM), lambda i: (i % per_seq, 0))] * 2
        args += list(rope_tabs)
    out_specs = [pl.BlockSpec((tm, wd), lambda i: (i, 0)) for wd, _ in PROJ_WIDTHS]
    out_shape = [jax.ShapeDtypeStruct((t_tokens, wd), dt) for wd, dt in PROJ_WIDTHS]
    aliases = {}
    n_aliased = 0
    if prev_cache is not None:
        nb = tm // seq
        n_aliased = len(prev_cache)
        for prev in prev_cache:
            aliases[len(args)] = len(out_shape)
            in_specs.append(pl.BlockSpec(memory_space=pl.ANY))
            args.append(prev)
            out_specs.append(pl.BlockSpec((nb, None) + prev.shape[2:], lambda i: (i, layer, 0, 0)))
            out_shape.append(jax.ShapeDtypeStruct(prev.shape, prev.dtype))
    return pl.pallas_call(
        functools.partial(_inproj_kernel, rope=rope, n_aliased=n_aliased),
        grid=(t_tokens // tm,),
        in_specs=in_specs,
        out_specs=out_specs,
        out_shape=out_shape,
        input_output_aliases=aliases,
        compiler_params=_params("arbitrary"),
        name="inproj_rope" if rope else "inproj_ctx",
    )(*args)


def _conv_body(u_ref, halo, w_ref, b_ref, lg_ref, lb_ref, o_ref, hp_ref, y_ref):
    rows = u_ref.shape[0]

    def glu(u):
        return u[:, :CONV_CH] * _sigmoid(u[:, CONV_CH:])

    zeros = jnp.zeros((CONV_HALO, CONV_CH), F32)
    if halo is not None:
        up_ref, un_ref, has_prev, has_next = halo
        hp_ref[0:CONV_HALO, :] = jnp.where(has_prev, glu(up_ref[...]), zeros)
        hp_ref[CONV_HALO + rows:, :] = jnp.where(has_next, glu(un_ref[...]), zeros)
    else:
        hp_ref[0:CONV_HALO, :] = zeros
        hp_ref[CONV_HALO + rows:, :] = zeros
    hp_ref[CONV_HALO:CONV_HALO + rows, :] = glu(u_ref[...])

    rchunk = 128
    base = CONV_HALO - CONV_WIDTH // 2
    groups = -(-(base + CONV_WIDTH) // SUBLANES)
    for c in range(CONV_CH // HEAD_DIM):
        cs = slice(c * HEAD_DIM, (c + 1) * HEAD_DIM)
        for r in range(rows // rchunk):
            acc = None
            for s in range(SUBLANES):
                part = None
                for q in range(groups):
                    k = q * SUBLANES + s - base
                    if 0 <= k < CONV_WIDTH:
                        r0 = r * rchunk + q * SUBLANES
                        term = hp_ref[r0:r0 + rchunk + SUBLANES, cs] * w_ref[k:k + 1, cs]
                        part = term if part is None else part + term
                part = part[s:s + rchunk]
                acc = part if acc is None else acc + part
            y_ref[r * rchunk:(r + 1) * rchunk, cs] = acc

    y = y_ref[...] + b_ref[...]
    mu = jnp.mean(y, axis=-1, keepdims=True)
    yc = y - mu
    var = jnp.mean(yc * yc, axis=-1, keepdims=True)
    z = yc * lax.rsqrt(var + NORM_EPS) * lg_ref[...] + lb_ref[...]
    o_ref[...] = (z * _sigmoid(z)).astype(BF16)


def _conv_kernel(*refs, blocks_per_seq):
    if blocks_per_seq > 1:
        u_ref, up_ref, un_ref = refs[:3]
        j = pl.program_id(0) % blocks_per_seq
        halo = (up_ref, un_ref, j > 0, j < blocks_per_seq - 1)
        rest = refs[3:]
    else:
        u_ref, halo, rest = refs[0], None, refs[1:]
    _conv_body(u_ref, halo, *rest)


def _conv(u, w, b, lg, lb, seq):
    t_tokens = u.shape[0]
    rows = CONV_BLOCK
    blocks_per_seq = seq // rows
    halo_per_block = rows // CONV_HALO
    n_halo = t_tokens // CONV_HALO
    const = lambda i: (0, 0)
    in_specs = [pl.BlockSpec((rows, 2 * CONV_CH), lambda i: (i, 0))]
    args = [u]
    if blocks_per_seq > 1:
        in_specs += [
            pl.BlockSpec((CONV_HALO, 2 * CONV_CH), lambda i: (jnp.maximum(i * halo_per_block - 1, 0), 0)),
            pl.BlockSpec((CONV_HALO, 2 * CONV_CH),
                         lambda i: (jnp.minimum((i + 1) * halo_per_block, n_halo - 1), 0)),
        ]
        args += [u, u]
    in_specs += [pl.BlockSpec((CONV_WIDTH, CONV_CH), const)] + [pl.BlockSpec((1, CONV_CH), const)] * 3
    args += [w, b, lg, lb]
    return pl.pallas_call(
        functools.partial(_conv_kernel, blocks_per_seq=blocks_per_seq),
        grid=(t_tokens // rows,),
        in_specs=in_specs,
        out_specs=pl.BlockSpec((rows, CONV_CH), lambda i: (i, 0)),
        out_shape=jax.ShapeDtypeStruct((t_tokens, CONV_CH), BF16),
        scratch_shapes=[pltpu.VMEM((rows + 2 * CONV_HALO, CONV_CH), F32),
                        pltpu.VMEM((rows, CONV_CH), F32)],
        compiler_params=_params("arbitrary"),
        name="conv_mixer",
    )(*args)


def _stack_heads(q, g):
    return jnp.concatenate(
        [q[:, (g * GROUP + j) * HEAD_DIM:(g * GROUP + j + 1) * HEAD_DIM] for j in range(GROUP)], axis=0)


def _scores(q, k):
    return lax.dot_general(q, k, (((1,), (1,)), ((), ())), preferred_element_type=F32)


def _sink_column(sink_ref, g, rows):
    return jnp.concatenate(
        [jnp.full((rows, 1), sink_ref[g * GROUP + j] * LOG2E, F32) for j in range(GROUP)], axis=0)


def _store_heads(o_ref, o, g, rows):
    for j in range(GROUP):
        c0 = (g * GROUP + j) * HEAD_DIM
        o_ref[:, c0:c0 + HEAD_DIM] = o[j * rows:(j + 1) * rows].astype(o_ref.dtype)


def _attention_jobs(jobs):
    n = len(jobs)
    pieces = [None] * n
    for t in range(-SCORE_LOOKAHEAD, n):
        if t + SCORE_LOOKAHEAD < n:
            pieces[t + SCORE_LOOKAHEAD] = jobs[t + SCORE_LOOKAHEAD][0]()
        if t < 0:
            continue
        _, snk, store = jobs[t]
        m = functools.reduce(jnp.maximum, [jnp.max(s, axis=-1, keepdims=True) for s, _ in pieces[t]])
        if snk is not None:
            m = jnp.maximum(m, snk)
        es = [jnp.exp2(s - m) for s, _ in pieces[t]]
        den = functools.reduce(jnp.add, [jnp.sum(e, axis=-1, keepdims=True) for e in es])
        if snk is not None:
            den = den + jnp.exp2(snk - m)
        o = functools.reduce(jnp.add, [jnp.dot(e.astype(BF16), v, preferred_element_type=F32)
                                       for e, (_, v) in zip(es, pieces[t])])
        store(o * (1.0 / den))
        pieces[t] = None


def _attn_ctx_kernel(sink_ref, qg_ref, kg_ref, vg_ref, qw_ref, kw_ref, vw_ref, og_ref, ow_ref, *, seq):
    jobs = []
    for b in range(qg_ref.shape[0] // seq):
        rs = slice(b * seq, (b + 1) * seq)
        for g in range(GLOB_KV):
            hs = slice(g * HEAD_DIM, (g + 1) * HEAD_DIM)

            def glob_scores(rs=rs, hs=hs, g=g):
                return [(_scores(_stack_heads(qg_ref[rs, :], g), kg_ref[rs, hs]), vg_ref[rs, hs])]

            def win_scores(rs=rs, hs=hs, g=g):
                return [(_scores(_stack_heads(qw_ref[rs, :], g), kw_ref[rs, hs]), vw_ref[rs, hs])]

            jobs.append((glob_scores, None, functools.partial(_store_heads, og_ref.at[rs, :], g=g, rows=seq)))
            jobs.append((win_scores, _sink_column(sink_ref, g, seq),
                         functools.partial(_store_heads, ow_ref.at[rs, :], g=g, rows=seq)))
    _attention_jobs(jobs)


def _attn_ctx(sink, qg, kg, vg, qw, kw, vw, seq):
    t_tokens = qg.shape[0]
    rows = 2 * seq
    spec = lambda wd: pl.BlockSpec((rows, wd), lambda i: (i, 0))
    return pl.pallas_call(
        functools.partial(_attn_ctx_kernel, seq=seq),
        grid=(t_tokens // rows,),
        in_specs=[pl.BlockSpec(memory_space=pltpu.SMEM),
                  spec(GLOB_Q), spec(GLOB_KVW), spec(GLOB_KVW), spec(WIN_Q), spec(WIN_KVW), spec(WIN_KVW)],
        out_specs=[spec(GLOB_Q), spec(WIN_Q)],
        out_shape=[jax.ShapeDtypeStruct((t_tokens, GLOB_Q), BF16),
                   jax.ShapeDtypeStruct((t_tokens, WIN_Q), BF16)],
        compiler_params=_params("arbitrary"),
        name="attn_ctx",
    )(sink, qg, kg, vg, qw, kw, vw)


def _attn_lat_kernel(sink_ref, qg_ref, kg_ref, vg_ref, cgk_ref, cgv_ref,
                     qw_ref, kw_ref, vw_ref, cwk_ref, cwv_ref, og_ref, ow_ref,
                     cgk_b, cgv_b, cwk_b, cwv_b, *, seq, qb):
    bi = pl.program_id(1)

    @pl.when(bi == 0)
    def _():
        for src, dst in ((cgk_ref, cgk_b), (cgv_ref, cgv_b), (cwk_ref, cwk_b), (cwv_ref, cwv_b)):
            past = dst.shape[0]
            kv = src.shape[0] // past
            for h in range(kv):
                dst[:, h * HEAD_DIM:(h + 1) * HEAD_DIM] = src[pl.ds(h, past, stride=kv), :].astype(BF16)

    span = QSUB + 2 * WINDOW
    jobs = []
    for sb in range(qb // QSUB):
        rs = slice(sb * QSUB, (sb + 1) * QSUB)
        q0 = bi * qb + sb * QSUB
        start = pl.multiple_of(jnp.clip(q0 - WINDOW, 0, seq - span), HEAD_DIM)
        qpos = q0 + lax.broadcasted_iota(jnp.int32, (QSUB, span), 0)
        kpos = start + lax.broadcasted_iota(jnp.int32, (QSUB, span), 1)
        band = jnp.where(jnp.abs(qpos - kpos) <= WINDOW, 0.0, NEG_INF)
        band = jnp.concatenate([band] * GROUP, axis=0)
        for g in range(GLOB_KV):
            hs = slice(g * HEAD_DIM, (g + 1) * HEAD_DIM)

            def glob_scores(rs=rs, hs=hs, g=g):
                q = _stack_heads(qg_ref[rs, :], g)
                return [(_scores(q, cgk_b[:, hs]), cgv_b[:, hs]), (_scores(q, kg_ref[:, hs]), vg_ref[:, hs])]

            def win_scores(rs=rs, hs=hs, g=g, start=start, band=band):
                q = _stack_heads(qw_ref[rs, :], g)
                s_loc = _scores(q, kw_ref[pl.ds(start, span), hs]) + band
                return [(_scores(q, cwk_b[:, hs]), cwv_b[:, hs]), (s_loc, vw_ref[pl.ds(start, span), hs])]

            jobs.append((glob_scores, None, functools.partial(_store_heads, og_ref.at[rs, :], g=g, rows=QSUB)))
            jobs.append((win_scores, _sink_column(sink_ref, g, QSUB),
                         functools.partial(_store_heads, ow_ref.at[rs, :], g=g, rows=QSUB)))
    _attention_jobs(jobs[0::2] + jobs[1::2])


def _attn_lat(sink, layer, qg, kg, vg, cgk, cgv, qw, kw, vw, cwk, cwv, seq):
    t_tokens = qg.shape[0]
    nbatch = t_tokens // seq
    qb = QSUB
    nqb = seq // qb
    past = cgk.shape[2] // GLOB_KV
    qspec = lambda wd: pl.BlockSpec((qb, wd), lambda b, i: (b * nqb + i, 0))
    kvspec = lambda wd: pl.BlockSpec((seq, wd), lambda b, i: (b, 0))
    cspec = lambda wd: pl.BlockSpec((None, None, past * (wd // HEAD_DIM), HEAD_DIM), lambda b, i: (b, layer, 0, 0))
    return pl.pallas_call(
        functools.partial(_attn_lat_kernel, seq=seq, qb=qb),
        grid=(nbatch, nqb),
        in_specs=[pl.BlockSpec(memory_space=pltpu.SMEM),
                  qspec(GLOB_Q), kvspec(GLOB_KVW), kvspec(GLOB_KVW), cspec(GLOB_KVW), cspec(GLOB_KVW),
                  qspec(WIN_Q), kvspec(WIN_KVW), kvspec(WIN_KVW), cspec(WIN_KVW), cspec(WIN_KVW)],
        out_specs=[qspec(GLOB_Q), qspec(WIN_Q)],
        out_shape=[jax.ShapeDtypeStruct((t_tokens, GLOB_Q), BF16),
                   jax.ShapeDtypeStruct((t_tokens, WIN_Q), BF16)],
        scratch_shapes=[pltpu.VMEM((past, wd), BF16) for wd in (GLOB_KVW, GLOB_KVW, WIN_KVW, WIN_KVW)],
        compiler_params=_params("arbitrary", "arbitrary"),
        name="attn_lat",
    )(sink, qg, kg, vg, cgk, cgv, qw, kw, vw, cwk, cwv)


def _outproj_kernel(x_ref, mod_ref, g_ref, yc_ref, yg_ref, yw_ref, w_ref, o_ref, h_ref, *, rchunk):
    c1 = CONV_CH
    c2 = CONV_CH + GLOB_Q
    for r in range(x_ref.shape[0] // rchunk):
        rs = slice(r * rchunk, (r + 1) * rchunk)
        mix = (jnp.dot(yc_ref[rs, :], w_ref[0:c1, :], preferred_element_type=F32)
               + jnp.dot(yg_ref[rs, :], w_ref[c1:c2, :], preferred_element_type=F32)
               + jnp.dot(yw_ref[rs, :], w_ref[c2:, :], preferred_element_type=F32))
        x1 = x_ref[rs, :] + mod_ref[2:3, :] * mix
        o_ref[rs, :] = x1
        h = _rms(x1, g_ref[...]) * (1.0 + mod_ref[4:5, :]) + mod_ref[3:4, :]
        h_ref[rs, :] = h.astype(BF16)


def _outproj(x, mod, row_of_tile, g_mlp, yc, yg, yw, w):
    t_tokens, d = x.shape
    tm = 512
    row = lambda wd: pl.BlockSpec((tm, wd), lambda i: (i, 0))
    return pl.pallas_call(
        functools.partial(_outproj_kernel, rchunk=256),
        grid=(t_tokens // tm,),
        in_specs=[row(d),
                  pl.BlockSpec((None, 6, d), lambda i: (row_of_tile(i, tm), 0, 0)),
                  pl.BlockSpec((1, d), lambda i: (0, 0)),
                  row(CONV_CH), row(GLOB_Q), row(WIN_Q),
                  pl.BlockSpec(w.shape, lambda i: (0, 0), pipeline_mode=pl.Buffered(1))],
        out_specs=[row(d), row(d)],
        out_shape=[jax.ShapeDtypeStruct((t_tokens, d), F32), jax.ShapeDtypeStruct((t_tokens, d), BF16)],
        compiler_params=_params("arbitrary"),
        name="outproj",
    )(x, mod, g_mlp, yc, yg, yw, w)


def _mlp_kernel(*refs, final_norm, n_cast):
    n_in = 5 + int(final_norm)
    x_ref, h_ref, mod_ref, w1_ref, w2_ref = refs[:5]
    cast_in = refs[n_in:n_in + n_cast]
    o_ref = refs[n_in + n_cast]
    cast_out = refs[n_in + n_cast + 1:]
    acc_ref = o_ref
    j = pl.program_id(1)
    last = pl.num_programs(1) - 1

    def partial_sum():
        a = jnp.maximum(jnp.dot(h_ref[...], w1_ref[...], preferred_element_type=F32), 0.0)
        part = jnp.dot((a * a).astype(BF16), w2_ref[...], preferred_element_type=F32)
        for src, dst in zip(cast_in, cast_out):
            dst[...] = src[...].astype(BF16)
        return part

    @pl.when(j == 0)
    def _():
        acc_ref[...] = partial_sum()

    @pl.when(jnp.logical_and(j > 0, j < last))
    def _():
        acc_ref[...] += partial_sum()

    @pl.when(j == last)
    def _():
        y = x_ref[...] + mod_ref[5:6, :] * (acc_ref[...] + partial_sum())
        if final_norm:
            y = _rms(y, refs[5][...])
        o_ref[...] = y


def _mlp(x, h, mod, row_of_tile, w1, w2, g_final, cast_jobs):
    t_tokens, d = x.shape
    d_ff = w1.shape[1]
    tm, tf = 512, 1024
    ni, nj = t_tokens // tm, d_ff // tf
    assert nj >= 2
    final_norm = g_final is not None
    in_specs = [
        pl.BlockSpec((tm, d), lambda i, j: (i, 0)),
        pl.BlockSpec((tm, d), lambda i, j: (i, 0)),
        pl.BlockSpec((None, 6, d), lambda i, j: (row_of_tile(i, tm), 0, 0)),
        pl.BlockSpec((d, tf), lambda i, j: (0, j)),
        pl.BlockSpec((tf, d), lambda i, j: (j, 0)),
    ]
    args = [x, h, mod, w1, w2]
    if final_norm:
        in_specs.append(pl.BlockSpec((1, d), lambda i, j: (0, 0)))
        args.append(g_final)
    out_specs = [pl.BlockSpec((tm, d), lambda i, j: (i, 0))]
    out_shape = [jax.ShapeDtypeStruct((t_tokens, d), F32)]
    for w, layer in cast_jobs:
        _, rows, cols = w.shape
        rps = rows // (ni * nj)
        assert rps * ni * nj == rows and rps % 16 == 0
        in_specs.append(pl.BlockSpec((None, rps, cols), lambda i, j, layer=layer: (layer, i * nj + j, 0)))
        args.append(w)
        out_specs.append(pl.BlockSpec((rps, cols), lambda i, j: (i * nj + j, 0)))
        out_shape.append(jax.ShapeDtypeStruct((rows, cols), BF16))
    return pl.pallas_call(
        functools.partial(_mlp_kernel, final_norm=final_norm, n_cast=len(cast_jobs)),
        grid=(ni, nj),
        in_specs=in_specs,
        out_specs=out_specs,
        out_shape=out_shape,
        compiler_params=_params("arbitrary", "arbitrary"),
        name="mlp_final" if final_norm else "mlp",
    )(*args)


def _rope_tables(n_tokens):
    rows = n_tokens // GRID_W
    r, col = jnp.meshgrid(jnp.arange(rows), jnp.arange(GRID_W), indexing='ij')
    r = r.reshape(-1).astype(F32)
    col = col.reshape(-1).astype(F32)
    n_freq = HEAD_DIM // 4
    inv = ROPE_THETA ** (-jnp.arange(n_freq, dtype=F32) / n_freq)
    ang_r = r[:, None] * inv
    ang_c = col[:, None] * inv
    ang = jnp.concatenate([ang_r, ang_r, ang_c, ang_c], axis=-1)
    sign = jnp.asarray(np.tile(np.repeat(np.array([-1.0, 1.0], np.float32), n_freq), 2))
    return jnp.cos(ang), jnp.sin(ang) * sign


def kernel(x_prompt, x_sample, cache_glob_k, cache_glob_v, cache_win_k, cache_win_v, c, c_ctx, w_ada, b_ada,
           g_attn, g_mlp, w_in, conv_w, conv_b, conv_ln_g, conv_ln_b, q_norm_g, k_norm_g, sink, w_out,
           w_mlp1, w_mlp2, g_final):
    batch, seq, d = x_prompt.shape
    dec_batch, dec_seq, _ = x_sample.shape
    depth = w_ada.shape[0]
    ctx_row = dec_batch
    assert dec_batch < MOD_ROWS

    cin = jnp.concatenate([c, c_ctx[None, :], jnp.zeros((MOD_ROWS - dec_batch - 1, d), F32)], axis=0)
    mods = _ada(cin, w_ada, b_ada).reshape(depth, MOD_ROWS, 6, d)
    rope_tabs = _rope_tables(dec_seq)

    w_in_b, w_out_b, w1_b, w2_b = (w[0].astype(BF16) for w in (w_in, w_out, w_mlp1, w_mlp2))
    caches = [a.reshape(dec_batch, depth, -1, HEAD_DIM) for a in (cache_glob_k, cache_glob_v, cache_win_k, cache_win_v)]

    ctx_rows = lambda i, tm: ctx_row
    lat_rows = lambda i, tm: (i * tm) // dec_seq

    xp = x_prompt.reshape(batch * seq, d)
    xs = x_sample.reshape(dec_batch * dec_seq, d)
    new_cache = [jnp.zeros((batch, depth, seq * kv, HEAD_DIM), F32) for kv in (GLOB_KV, GLOB_KV, WIN_KV, WIN_KV)]
    for l in range(depth):
        mod = mods[l]
        row = lambda a: a[l].reshape(1, -1)
        g_fin = g_final.reshape(1, d) if l == depth - 1 else None
        conv_args = (conv_w[l], row(conv_b), row(conv_ln_g), row(conv_ln_b))
        more = l + 1 < depth
        cast_ctx = [(w_in, l + 1), (w_out, l + 1), (w_mlp1, l + 1)] if more else []
        cast_lat = [(w_mlp2, l + 1)] if more else []

        u, qg, kg, vg, qw, kw, vw, *new_cache = _inproj(
            xp, mod, ctx_rows, row(g_attn), w_in_b, l, row(q_norm_g), row(k_norm_g), None, seq, new_cache)
        yc = _conv(u, *conv_args, seq)
        yg, yw = _attn_ctx(sink[l], qg, kg, vg, qw, kw, vw, seq)
        xp, hp = _outproj(xp, mod, ctx_rows, row(g_mlp), yc, yg, yw, w_out_b)
        xp, *next_ctx = _mlp(xp, hp, mod, ctx_rows, w1_b, w2_b, g_fin, cast_ctx)

        u, qg, kg, vg, qw, kw, vw = _inproj(
            xs, mod, lat_rows, row(g_attn), w_in_b, l, row(q_norm_g), row(k_norm_g), rope_tabs, dec_seq, None)
        yc = _conv(u, *conv_args, dec_seq)
        yg, yw = _attn_lat(sink[l], l, qg, kg, vg, caches[0], caches[1], qw, kw, vw, caches[2], caches[3],
                           dec_seq)
        xs, hs = _outproj(xs, mod, lat_rows, row(g_mlp), yc, yg, yw, w_out_b)
        xs, *next_lat = _mlp(xs, hs, mod, lat_rows, w1_b, w2_b, g_fin, cast_lat)
        if more:
            (w_in_b, w_out_b, w1_b), (w2_b,) = next_ctx, next_lat

    outs = [a.reshape(batch, depth, seq, -1, HEAD_DIM) for a in new_cache]
    return (xp.reshape(batch, seq, d), xs.reshape(dec_batch, dec_seq, d), *outs)
```

```python
import functools

import jax
import jax.numpy as jnp
import numpy as np
from jax import lax
from jax.experimental import pallas as pl
from jax.experimental.pallas import tpu as pltpu

HEAD_DIM = 128
CONV_CH = 512
CONV_WIDTH = 31
GLOB_HEADS = 6
GLOB_KV = 2
WIN_HEADS = 6
WIN_KV = 2
WINDOW = 128
GRID_W = 64
ROPE_THETA = 10000.0
NORM_EPS = 1e-6
NEG_INF = -1e30

GLOB_Q = GLOB_HEADS * HEAD_DIM
GLOB_KVW = GLOB_KV * HEAD_DIM
WIN_Q = WIN_HEADS * HEAD_DIM
WIN_KVW = WIN_KV * HEAD_DIM
GROUP = GLOB_HEADS // GLOB_KV
C_U = 0
C_QG = 2 * CONV_CH
C_KG = C_QG + GLOB_Q
C_VG = C_KG + GLOB_KVW
C_QW = C_VG + GLOB_KVW
C_KW = C_QW + WIN_Q
C_VW = C_KW + WIN_KVW
IN_WIDTH = C_VW + WIN_KVW

LOG2E = float(np.log2(np.e))
Q_PRESCALE = HEAD_DIM ** -0.5 * LOG2E

MOD_ROWS = 16
CONV_HALO = 16
SUBLANES = 8
CONV_BLOCK = 256
QSUB = 128
SCORE_LOOKAHEAD = 1
V7X_VMEM_LIMIT = 56 * 1024 * 1024

BF16 = jnp.bfloat16
F32 = jnp.float32


def _params(*sem):
    return pltpu.CompilerParams(dimension_semantics=sem, vmem_limit_bytes=V7X_VMEM_LIMIT)


def _sigmoid(x):
    return 1.0 / (1.0 + jnp.exp(-x))


def _rms(x, gain):
    return x * lax.rsqrt(jnp.mean(x * x, axis=-1, keepdims=True) + NORM_EPS) * gain


def _ada_kernel(c_ref, w_ref, b_ref, o_ref):
    c = c_ref[...]
    a = (c * _sigmoid(c)).astype(BF16)
    o_ref[...] = jnp.dot(a, w_ref[...].astype(BF16), preferred_element_type=F32) + b_ref[...]


def _ada(cin, w_ada, b_ada):
    depth, d, n = w_ada.shape
    tn = 2048
    return pl.pallas_call(
        _ada_kernel,
        grid=(depth, n // tn),
        in_specs=[
            pl.BlockSpec((MOD_ROWS, d), lambda l, j: (0, 0)),
            pl.BlockSpec((None, d, tn), lambda l, j: (l, 0, j)),
            pl.BlockSpec((None, 1, tn), lambda l, j: (l, 0, j)),
        ],
        out_specs=pl.BlockSpec((None, MOD_ROWS, tn), lambda l, j: (l, 0, j)),
        out_shape=jax.ShapeDtypeStruct((depth, MOD_ROWS, n), F32),
        compiler_params=_params("arbitrary", "arbitrary"),
        name="ada_mod",
    )(cin, w_ada, b_ada.reshape(depth, 1, n))


def _store_cache(ref, kv, cols, t):
    nb, rows, _ = ref.shape
    seq = rows // kv
    heads = range(kv)[slice(cols.start and cols.start // HEAD_DIM, cols.stop and cols.stop // HEAD_DIM)]
    for b in range(nb):
        for i, h in enumerate(heads):
            ref[b, pl.ds(h, seq, stride=kv), :] = t[b * seq:(b + 1) * seq, i * HEAD_DIM:(i + 1) * HEAD_DIM]


def _project(x_ref, mod_ref, g_ref, w_ref, qn_ref, kn_ref, rope_refs, outs, caches):
    u_ref, qg_ref, kg_ref, vg_ref, qw_ref, kw_ref, vw_ref = outs
    h = _rms(x_ref[...], g_ref[...]) * (1.0 + mod_ref[1:2, :]) + mod_ref[0:1, :]
    hb = h.astype(BF16)

    def proj(c0, c1):
        return jnp.dot(hb, w_ref[:, c0:c1], preferred_element_type=F32)

    if rope_refs is not None:
        cos = rope_refs[0][...]
        sin = rope_refs[1][...]
        lane = lax.broadcasted_iota(jnp.int32, cos.shape, 1)
        take_next = ((lane // (HEAD_DIM // 4)) % 2) == 0

        def rot(t):
            r = jnp.where(take_next, pltpu.roll(t, HEAD_DIM - HEAD_DIM // 4, 1),
                          pltpu.roll(t, HEAD_DIM // 4, 1))
            return t * cos + r * sin
    else:
        def rot(t):
            return t

    def heads(p, n):
        return [(slice(i * HEAD_DIM, (i + 1) * HEAD_DIM), p[:, i * HEAD_DIM:(i + 1) * HEAD_DIM])
                for i in range(n)]

    u_ref[...] = proj(C_U, C_QG)

    qn = qn_ref[...]
    kn = kn_ref[...]
    for cols, t in heads(proj(C_QG, C_KG), GLOB_HEADS):
        qg_ref[:, cols] = (rot(_rms(t, qn)) * Q_PRESCALE).astype(BF16)
    for cols, t in heads(proj(C_KG, C_VG), GLOB_KV):
        t = _rms(t, kn)
        if caches is not None:
            _store_cache(caches[0], GLOB_KV, cols, t)
        kg_ref[:, cols] = rot(t).astype(BF16)
    p = proj(C_VG, C_QW)
    if caches is not None:
        _store_cache(caches[1], GLOB_KV, slice(None), p)
    vg_ref[...] = p.astype(BF16)
    for cols, t in heads(proj(C_QW, C_KW), WIN_HEADS):
        qw_ref[:, cols] = (rot(t) * Q_PRESCALE).astype(BF16)
    p = proj(C_KW, C_VW)
    if caches is not None:
        _store_cache(caches[2], WIN_KV, slice(None), p)
    for cols, t in heads(p, WIN_KV):
        kw_ref[:, cols] = rot(t).astype(BF16)
    p = proj(C_VW, IN_WIDTH)
    if caches is not None:
        _store_cache(caches[3], WIN_KV, slice(None), p)
    vw_ref[...] = p.astype(BF16)


PROJ_WIDTHS = ((C_QG - C_U, F32), (GLOB_Q, BF16), (GLOB_KVW, BF16), (GLOB_KVW, BF16),
               (WIN_Q, BF16), (WIN_KVW, BF16), (WIN_KVW, BF16))


def _inproj_kernel(*refs, rope, n_aliased):
    x_ref, mod_ref, g_ref, w_ref, qn_ref, kn_ref = refs[:6]
    pos = 6
    rope_refs = None
    if rope:
        rope_refs = refs[pos:pos + 2]
        pos += 2
    pos += n_aliased
    outs = refs[pos:pos + len(PROJ_WIDTHS)]
    caches = refs[pos + len(PROJ_WIDTHS):] or None
    _project(x_ref, mod_ref, g_ref, w_ref, qn_ref, kn_ref, rope_refs, outs, caches)


def _inproj(x, mod, row_of_tile, g, w, layer, qn, kn, rope_tabs, seq, prev_cache):
    t_tokens, d = x.shape
    tm = 512
    rope = rope_tabs is not None
    const = lambda i: (0, 0)
    in_specs = [
        pl.BlockSpec((tm, d), lambda i: (i, 0)),
        pl.BlockSpec((None, 6, d), lambda i: (row_of_tile(i, tm), 0, 0)),
        pl.BlockSpec((1, d), const),
        pl.BlockSpec((d, IN_WIDTH), const, pipeline_mode=pl.Buffered(1)),
        pl.BlockSpec((1, HEAD_DIM), const),
        pl.BlockSpec((1, HEAD_DIM), const),
    ]
    args = [x, mod, g, w, qn, kn]
    if rope:
        per_seq = seq // tm
        in_specs += [pl.BlockSpec((tm, HEAD_DIM), lambda i: (i % per_seq, 0))] * 2
        args += list(rope_tabs)
    out_specs = [pl.BlockSpec((tm, wd), lambda i: (i, 0)) for wd, _ in PROJ_WIDTHS]
    out_shape = [jax.ShapeDtypeStruct((t_tokens, wd), dt) for wd, dt in PROJ_WIDTHS]
    aliases = {}
    n_aliased = 0
    if prev_cache is not None:
        nb = tm // seq
        n_aliased = len(prev_cache)
        for prev in prev_cache:
            aliases[len(args)] = len(out_shape)
            in_specs.append(pl.BlockSpec(memory_space=pl.ANY))
            args.append(prev)
            out_specs.append(pl.BlockSpec((nb, None) + prev.shape[2:], lambda i: (i, layer, 0, 0)))
            out_shape.append(jax.ShapeDtypeStruct(prev.shape, prev.dtype))
    return pl.pallas_call(
        functools.partial(_inproj_kernel, rope=rope, n_aliased=n_aliased),
        grid=(t_tokens // tm,),
        in_specs=in_specs,
        out_specs=out_specs,
        out_shape=out_shape,
        input_output_aliases=aliases,
        compiler_params=_params("arbitrary"),
        name="inproj_rope" if rope else "inproj_ctx",
    )(*args)


def _conv_body(u_ref, halo, w_ref, b_ref, lg_ref, lb_ref, o_ref, hp_ref, y_ref):
    rows = u_ref.shape[0]

    def glu(u):
        return u[:, :CONV_CH] * _sigmoid(u[:, CONV_CH:])

    zeros = jnp.zeros((CONV_HALO, CONV_CH), F32)
    if halo is not None:
        up_ref, un_ref, has_prev, has_next = halo
        hp_ref[0:CONV_HALO, :] = jnp.where(has_prev, glu(up_ref[...]), zeros)
        hp_ref[CONV_HALO + rows:, :] = jnp.where(has_next, glu(un_ref[...]), zeros)
    else:
        hp_ref[0:CONV_HALO, :] = zeros
        hp_ref[CONV_HALO + rows:, :] = zeros
    hp_ref[CONV_HALO:CONV_HALO + rows, :] = glu(u_ref[...])

    rchunk = 128
    base = CONV_HALO - CONV_WIDTH // 2
    groups = -(-(base + CONV_WIDTH) // SUBLANES)
    for c in range(CONV_CH // HEAD_DIM):
        cs = slice(c * HEAD_DIM, (c + 1) * HEAD_DIM)
        for r in range(rows // rchunk):
            acc = None
            for s in range(SUBLANES):
                part = None
                for q in range(groups):
                    k = q * SUBLANES + s - base
                    if 0 <= k < CONV_WIDTH:
                        r0 = r * rchunk + q * SUBLANES
                        term = hp_ref[r0:r0 + rchunk + SUBLANES, cs] * w_ref[k:k + 1, cs]
                        part = term if part is None else part + term
                part = part[s:s + rchunk]
                acc = part if acc is None else acc + part
            y_ref[r * rchunk:(r + 1) * rchunk, cs] = acc

    y = y_ref[...] + b_ref[...]
    mu = jnp.mean(y, axis=-1, keepdims=True)
    yc = y - mu
    var = jnp.mean(yc * yc, axis=-1, keepdims=True)
    z = yc * lax.rsqrt(var + NORM_EPS) * lg_ref[...] + lb_ref[...]
    o_ref[...] = (z * _sigmoid(z)).astype(BF16)


def _conv_kernel(*refs, blocks_per_seq):
    if blocks_per_seq > 1:
        u_ref, up_ref, un_ref = refs[:3]
        j = pl.program_id(0) % blocks_per_seq
        halo = (up_ref, un_ref, j > 0, j < blocks_per_seq - 1)
        rest = refs[3:]
    else:
        u_ref, halo, rest = refs[0], None, refs[1:]
    _conv_body(u_ref, halo, *rest)


def _conv(u, w, b, lg, lb, seq):
    t_tokens = u.shape[0]
    rows = CONV_BLOCK
    blocks_per_seq = seq // rows
    halo_per_block = rows // CONV_HALO
    n_halo = t_tokens // CONV_HALO
    const = lambda i: (0, 0)
    in_specs = [pl.BlockSpec((rows, 2 * CONV_CH), lambda i: (i, 0))]
    args = [u]
    if blocks_per_seq > 1:
        in_specs += [
            pl.BlockSpec((CONV_HALO, 2 * CONV_CH), lambda i: (jnp.maximum(i * halo_per_block - 1, 0), 0)),
            pl.BlockSpec((CONV_HALO, 2 * CONV_CH),
                         lambda i: (jnp.minimum((i + 1) * halo_per_block, n_halo - 1), 0)),
        ]
        args += [u, u]
    in_specs += [pl.BlockSpec((CONV_WIDTH, CONV_CH), const)] + [pl.BlockSpec((1, CONV_CH), const)] * 3
    args += [w, b, lg, lb]
    return pl.pallas_call(
        functools.partial(_conv_kernel, blocks_per_seq=blocks_per_seq),
        grid=(t_tokens // rows,),
        in_specs=in_specs,
        out_specs=pl.BlockSpec((rows, CONV_CH), lambda i: (i, 0)),
        out_shape=jax.ShapeDtypeStruct((t_tokens, CONV_CH), BF16),
        scratch_shapes=[pltpu.VMEM((rows + 2 * CONV_HALO, CONV_CH), F32),
                        pltpu.VMEM((rows, CONV_CH), F32)],
        compiler_params=_params("arbitrary"),
        name="conv_mixer",
    )(*args)


def _stack_heads(q, g):
    return jnp.concatenate(
        [q[:, (g * GROUP + j) * HEAD_DIM:(g * GROUP + j + 1) * HEAD_DIM] for j in range(GROUP)], axis=0)


def _scores(q, k):
    return lax.dot_general(q, k, (((1,), (1,)), ((), ())), preferred_element_type=F32)


def _sink_column(sink_ref, g, rows):
    return jnp.concatenate(
        [jnp.full((rows, 1), sink_ref[g * GROUP + j] * LOG2E, F32) for j in range(GROUP)], axis=0)


def _store_heads(o_ref, o, g, rows):
    for j in range(GROUP):
        c0 = (g * GROUP + j) * HEAD_DIM
        o_ref[:, c0:c0 + HEAD_DIM] = o[j * rows:(j + 1) * rows].astype(o_ref.dtype)


def _attention_jobs(jobs):
    n = len(jobs)
    pieces = [None] * n
    for t in range(-SCORE_LOOKAHEAD, n):
        if t + SCORE_LOOKAHEAD < n:
            pieces[t + SCORE_LOOKAHEAD] = jobs[t + SCORE_LOOKAHEAD][0]()
        if t < 0:
            continue
        _, snk, store = jobs[t]
        m = functools.reduce(jnp.maximum, [jnp.max(s, axis=-1, keepdims=True) for s, _ in pieces[t]])
        if snk is not None:
            m = jnp.maximum(m, snk)
        es = [jnp.exp2(s - m) for s, _ in pieces[t]]
        den = functools.reduce(jnp.add, [jnp.sum(e, axis=-1, keepdims=True) for e in es])
        if snk is not None:
            den = den + jnp.exp2(snk - m)
        o = functools.reduce(jnp.add, [jnp.dot(e.astype(BF16), v, preferred_element_type=F32)
                                       for e, (_, v) in zip(es, pieces[t])])
        store(o * (1.0 / den))
        pieces[t] = None


def _attn_ctx_kernel(sink_ref, qg_ref, kg_ref, vg_ref, qw_ref, kw_ref, vw_ref, og_ref, ow_ref, *, seq):
    jobs = []
    for b in range(qg_ref.shape[0] // seq):
        rs = slice(b * seq, (b + 1) * seq)
        for g in range(GLOB_KV):
            hs = slice(g * HEAD_DIM, (g + 1) * HEAD_DIM)

            def glob_scores(rs=rs, hs=hs, g=g):
                return [(_scores(_stack_heads(qg_ref[rs, :], g), kg_ref[rs, hs]), vg_ref[rs, hs])]

            def win_scores(rs=rs, hs=hs, g=g):
                return [(_scores(_stack_heads(qw_ref[rs, :], g), kw_ref[rs, hs]), vw_ref[rs, hs])]

            jobs.append((glob_scores, None, functools.partial(_store_heads, og_ref.at[rs, :], g=g, rows=seq)))
            jobs.append((win_scores, _sink_column(sink_ref, g, seq),
                         functools.partial(_store_heads, ow_ref.at[rs, :], g=g, rows=seq)))
    _attention_jobs(jobs)


def _attn_ctx(sink, qg, kg, vg, qw, kw, vw, seq):
    t_tokens = qg.shape[0]
    rows = 8 * seq
    spec = lambda wd: pl.BlockSpec((rows, wd), lambda i: (i, 0))
    return pl.pallas_call(
        functools.partial(_attn_ctx_kernel, seq=seq),
        grid=(t_tokens // rows,),
        in_specs=[pl.BlockSpec(memory_space=pltpu.SMEM),
                  spec(GLOB_Q), spec(GLOB_KVW), spec(GLOB_KVW), spec(WIN_Q), spec(WIN_KVW), spec(WIN_KVW)],
        out_specs=[spec(GLOB_Q), spec(WIN_Q)],
        out_shape=[jax.ShapeDtypeStruct((t_tokens, GLOB_Q), BF16),
                   jax.ShapeDtypeStruct((t_tokens, WIN_Q), BF16)],
        compiler_params=_params("arbitrary"),
        name="attn_ctx",
    )(sink, qg, kg, vg, qw, kw, vw)


def _attn_lat_kernel(sink_ref, qg_ref, kg_ref, vg_ref, cgk_ref, cgv_ref,
                     qw_ref, kw_ref, vw_ref, cwk_ref, cwv_ref, og_ref, ow_ref,
                     cgk_b, cgv_b, cwk_b, cwv_b, *, seq, qb):
    bi = pl.program_id(1)

    @pl.when(bi == 0)
    def _():
        for src, dst in ((cgk_ref, cgk_b), (cgv_ref, cgv_b), (cwk_ref, cwk_b), (cwv_ref, cwv_b)):
            past = dst.shape[0]
            kv = src.shape[0] // past
            for h in range(kv):
                dst[:, h * HEAD_DIM:(h + 1) * HEAD_DIM] = src[pl.ds(h, past, stride=kv), :].astype(BF16)

    span = QSUB + 2 * WINDOW
    jobs = []
    for sb in range(qb // QSUB):
        rs = slice(sb * QSUB, (sb + 1) * QSUB)
        q0 = bi * qb + sb * QSUB
        start = pl.multiple_of(jnp.clip(q0 - WINDOW, 0, seq - span), HEAD_DIM)
        qpos = q0 + lax.broadcasted_iota(jnp.int32, (QSUB, span), 0)
        kpos = start + lax.broadcasted_iota(jnp.int32, (QSUB, span), 1)
        band = jnp.where(jnp.abs(qpos - kpos) <= WINDOW, 0.0, NEG_INF)
        band = jnp.concatenate([band] * GROUP, axis=0)
        for g in range(GLOB_KV):
            hs = slice(g * HEAD_DIM, (g + 1) * HEAD_DIM)

            def glob_scores(rs=rs, hs=hs, g=g):
                q = _stack_heads(qg_ref[rs, :], g)
                return [(_scores(q, cgk_b[:, hs]), cgv_b[:, hs]), (_scores(q, kg_ref[:, hs]), vg_ref[:, hs])]

            def win_scores(rs=rs, hs=hs, g=g, start=start, band=band):
                q = _stack_heads(qw_ref[rs, :], g)
                s_loc = _scores(q, kw_ref[pl.ds(start, span), hs]) + band
                return [(_scores(q, cwk_b[:, hs]), cwv_b[:, hs]), (s_loc, vw_ref[pl.ds(start, span), hs])]

            jobs.append((glob_scores, None, functools.partial(_store_heads, og_ref.at[rs, :], g=g, rows=QSUB)))
            jobs.append((win_scores, _sink_column(sink_ref, g, QSUB),
                         functools.partial(_store_heads, ow_ref.at[rs, :], g=g, rows=QSUB)))
    _attention_jobs(jobs[0::2] + jobs[1::2])


def _attn_lat(sink, layer, qg, kg, vg, cgk, cgv, qw, kw, vw, cwk, cwv, seq):
    t_tokens = qg.shape[0]
    nbatch = t_tokens // seq
    qb = QSUB
    nqb = seq // qb
    past = cgk.shape[2] // GLOB_KV
    qspec = lambda wd: pl.BlockSpec((qb, wd), lambda b, i: (b * nqb + i, 0))
    kvspec = lambda wd: pl.BlockSpec((seq, wd), lambda b, i: (b, 0))
    cspec = lambda wd: pl.BlockSpec((None, None, past * (wd // HEAD_DIM), HEAD_DIM), lambda b, i: (b, layer, 0, 0))
    return pl.pallas_call(
        functools.partial(_attn_lat_kernel, seq=seq, qb=qb),
        grid=(nbatch, nqb),
        in_specs=[pl.BlockSpec(memory_space=pltpu.SMEM),
                  qspec(GLOB_Q), kvspec(GLOB_KVW), kvspec(GLOB_KVW), cspec(GLOB_KVW), cspec(GLOB_KVW),
                  qspec(WIN_Q), kvspec(WIN_KVW), kvspec(WIN_KVW), cspec(WIN_KVW), cspec(WIN_KVW)],
        out_specs=[qspec(GLOB_Q), qspec(WIN_Q)],
        out_shape=[jax.ShapeDtypeStruct((t_tokens, GLOB_Q), BF16),
                   jax.ShapeDtypeStruct((t_tokens, WIN_Q), BF16)],
        scratch_shapes=[pltpu.VMEM((past, wd), BF16) for wd in (GLOB_KVW, GLOB_KVW, WIN_KVW, WIN_KVW)],
        compiler_params=_params("arbitrary", "arbitrary"),
        name="attn_lat",
    )(sink, qg, kg, vg, cgk, cgv, qw, kw, vw, cwk, cwv)


def _outproj_kernel(x_ref, mod_ref, g_ref, yc_ref, yg_ref, yw_ref, w_ref, o_ref, h_ref, *, rchunk):
    c1 = CONV_CH
    c2 = CONV_CH + GLOB_Q
    for r in range(x_ref.shape[0] // rchunk):
        rs = slice(r * rchunk, (r + 1) * rchunk)
        mix = (jnp.dot(yc_ref[rs, :], w_ref[0:c1, :], preferred_element_type=F32)
               + jnp.dot(yg_ref[rs, :], w_ref[c1:c2, :], preferred_element_type=F32)
               + jnp.dot(yw_ref[rs, :], w_ref[c2:, :], preferred_element_type=F32))
        x1 = x_ref[rs, :] + mod_ref[2:3, :] * mix
        o_ref[rs, :] = x1
        h = _rms(x1, g_ref[...]) * (1.0 + mod_ref[4:5, :]) + mod_ref[3:4, :]
        h_ref[rs, :] = h.astype(BF16)


def _outproj(x, mod, row_of_tile, g_mlp, yc, yg, yw, w):
    t_tokens, d = x.shape
    tm = 512
    row = lambda wd: pl.BlockSpec((tm, wd), lambda i: (i, 0))
    return pl.pallas_call(
        functools.partial(_outproj_kernel, rchunk=256),
        grid=(t_tokens // tm,),
        in_specs=[row(d),
                  pl.BlockSpec((None, 6, d), lambda i: (row_of_tile(i, tm), 0, 0)),
                  pl.BlockSpec((1, d), lambda i: (0, 0)),
                  row(CONV_CH), row(GLOB_Q), row(WIN_Q),
                  pl.BlockSpec(w.shape, lambda i: (0, 0), pipeline_mode=pl.Buffered(1))],
        out_specs=[row(d), row(d)],
        out_shape=[jax.ShapeDtypeStruct((t_tokens, d), F32), jax.ShapeDtypeStruct((t_tokens, d), BF16)],
        compiler_params=_params("arbitrary"),
        name="outproj",
    )(x, mod, g_mlp, yc, yg, yw, w)


def _mlp_kernel(*refs, final_norm, n_cast):
    n_in = 5 + int(final_norm)
    x_ref, h_ref, mod_ref, w1_ref, w2_ref = refs[:5]
    cast_in = refs[n_in:n_in + n_cast]
    o_ref = refs[n_in + n_cast]
    cast_out = refs[n_in + n_cast + 1:]
    acc_ref = o_ref
    j = pl.program_id(1)
    last = pl.num_programs(1) - 1

    def partial_sum():
        a = jnp.maximum(jnp.dot(h_ref[...], w1_ref[...], preferred_element_type=F32), 0.0)
        part = jnp.dot((a * a).astype(BF16), w2_ref[...], preferred_element_type=F32)
        for src, dst in zip(cast_in, cast_out):
            dst[...] = src[...].astype(BF16)
        return part

    @pl.when(j == 0)
    def _():
        acc_ref[...] = partial_sum()

    @pl.when(jnp.logical_and(j > 0, j < last))
    def _():
        acc_ref[...] += partial_sum()

    @pl.when(j == last)
    def _():
        y = x_ref[...] + mod_ref[5:6, :] * (acc_ref[...] + partial_sum())
        if final_norm:
            y = _rms(y, refs[5][...])
        o_ref[...] = y


def _mlp(x, h, mod, row_of_tile, w1, w2, g_final, cast_jobs):
    t_tokens, d = x.shape
    d_ff = w1.shape[1]
    tm, tf = 512, 1024
    ni, nj = t_tokens // tm, d_ff // tf
    assert nj >= 2
    final_norm = g_final is not None
    in_specs = [
        pl.BlockSpec((tm, d), lambda i, j: (i, 0)),
        pl.BlockSpec((tm, d), lambda i, j: (i, 0)),
        pl.BlockSpec((None, 6, d), lambda i, j: (row_of_tile(i, tm), 0, 0)),
        pl.BlockSpec((d, tf), lambda i, j: (0, j)),
        pl.BlockSpec((tf, d), lambda i, j: (j, 0)),
    ]
    args = [x, h, mod, w1, w2]
    if final_norm:
        in_specs.append(pl.BlockSpec((1, d), lambda i, j: (0, 0)))
        args.append(g_final)
    out_specs = [pl.BlockSpec((tm, d), lambda i, j: (i, 0))]
    out_shape = [jax.ShapeDtypeStruct((t_tokens, d), F32)]
    for w, layer in cast_jobs:
        _, rows, cols = w.shape
        rps = rows // (ni * nj)
        assert rps * ni * nj == rows and rps % 16 == 0
        in_specs.append(pl.BlockSpec((None, rps, cols), lambda i, j, layer=layer: (layer, i * nj + j, 0)))
        args.append(w)
        out_specs.append(pl.BlockSpec((rps, cols), lambda i, j: (i * nj + j, 0)))
        out_shape.append(jax.ShapeDtypeStruct((rows, cols), BF16))
    return pl.pallas_call(
        functools.partial(_mlp_kernel, final_norm=final_norm, n_cast=len(cast_jobs)),
        grid=(ni, nj),
        in_specs=in_specs,
        out_specs=out_specs,
        out_shape=out_shape,
        compiler_params=_params("arbitrary", "arbitrary"),
        name="mlp_final" if final_norm else "mlp",
    )(*args)


def _rope_tables(n_tokens):
    rows = n_tokens // GRID_W
    r, col = jnp.meshgrid(jnp.arange(rows), jnp.arange(GRID_W), indexing='ij')
    r = r.reshape(-1).astype(F32)
    col = col.reshape(-1).astype(F32)
    n_freq = HEAD_DIM // 4
    inv = ROPE_THETA ** (-jnp.arange(n_freq, dtype=F32) / n_freq)
    ang_r = r[:, None] * inv
    ang_c = col[:, None] * inv
    ang = jnp.concatenate([ang_r, ang_r, ang_c, ang_c], axis=-1)
    sign = jnp.asarray(np.tile(np.repeat(np.array([-1.0, 1.0], np.float32), n_freq), 2))
    return jnp.cos(ang), jnp.sin(ang) * sign


def kernel(x_prompt, x_sample, cache_glob_k, cache_glob_v, cache_win_k, cache_win_v, c, c_ctx, w_ada, b_ada,
           g_attn, g_mlp, w_in, conv_w, conv_b, conv_ln_g, conv_ln_b, q_norm_g, k_norm_g, sink, w_out,
           w_mlp1, w_mlp2, g_final):
    batch, seq, d = x_prompt.shape
    dec_batch, dec_seq, _ = x_sample.shape
    depth = w_ada.shape[0]
    ctx_row = dec_batch
    assert dec_batch < MOD_ROWS

    cin = jnp.concatenate([c, c_ctx[None, :], jnp.zeros((MOD_ROWS - dec_batch - 1, d), F32)], axis=0)
    mods = _ada(cin, w_ada, b_ada).reshape(depth, MOD_ROWS, 6, d)
    rope_tabs = _rope_tables(dec_seq)

    w_in_b, w_out_b, w1_b, w2_b = (w[0].astype(BF16) for w in (w_in, w_out, w_mlp1, w_mlp2))
    caches = [a.reshape(dec_batch, depth, -1, HEAD_DIM) for a in (cache_glob_k, cache_glob_v, cache_win_k, cache_win_v)]

    ctx_rows = lambda i, tm: ctx_row
    lat_rows = lambda i, tm: (i * tm) // dec_seq

    xp = x_prompt.reshape(batch * seq, d)
    xs = x_sample.reshape(dec_batch * dec_seq, d)
    new_cache = [jnp.zeros((batch, depth, seq * kv, HEAD_DIM), F32) for kv in (GLOB_KV, GLOB_KV, WIN_KV, WIN_KV)]
    for l in range(depth):
        mod = mods[l]
        row = lambda a: a[l].reshape(1, -1)
        g_fin = g_final.reshape(1, d) if l == depth - 1 else None
        conv_args = (conv_w[l], row(conv_b), row(conv_ln_g), row(conv_ln_b))
        more = l + 1 < depth
        cast_ctx = [(w_in, l + 1), (w_out, l + 1), (w_mlp1, l + 1)] if more else []
        cast_lat = [(w_mlp2, l + 1)] if more else []

        u, qg, kg, vg, qw, kw, vw, *new_cache = _inproj(
            xp, mod, ctx_rows, row(g_attn), w_in_b, l, row(q_norm_g), row(k_norm_g), None, seq, new_cache)
        yc = _conv(u, *conv_args, seq)
        yg, yw = _attn_ctx(sink[l], qg, kg, vg, qw, kw, vw, seq)
        xp, hp = _outproj(xp, mod, ctx_rows, row(g_mlp), yc, yg, yw, w_out_b)
        xp, *next_ctx = _mlp(xp, hp, mod, ctx_rows, w1_b, w2_b, g_fin, cast_ctx)

        u, qg, kg, vg, qw, kw, vw = _inproj(
            xs, mod, lat_rows, row(g_attn), w_in_b, l, row(q_norm_g), row(k_norm_g), rope_tabs, dec_seq, None)
        yc = _conv(u, *conv_args, dec_seq)
        yg, yw = _attn_lat(sink[l], l, qg, kg, vg, caches[0], caches[1], qw, kw, vw, caches[2], caches[3],
                           dec_seq)
        xs, hs = _outproj(xs, mod, lat_rows, row(g_mlp), yc, yg, yw, w_out_b)
        xs, *next_lat = _mlp(xs, hs, mod, lat_rows, w1_b, w2_b, g_fin, cast_lat)
        if more:
            (w_in_b, w_out_b, w1_b), (w2_b,) = next_ctx, next_lat

    outs = [a.reshape(batch, depth, seq, -1, HEAD_DIM) for a in new_cache]
    return (xp.reshape(batch, seq, d), xs.reshape(dec_batch, dec_seq, d), *outs)
```

```python
import functools

import jax
import jax.numpy as jnp
import numpy as np
from jax import lax
from jax.experimental import pallas as pl
from jax.experimental.pallas import tpu as pltpu

HEAD_DIM = 128
CONV_CH = 512
CONV_WIDTH = 31
GLOB_HEADS = 6
GLOB_KV = 2
WIN_HEADS = 6
WIN_KV = 2
WINDOW = 128
GRID_W = 64
ROPE_THETA = 10000.0
NORM_EPS = 1e-6
NEG_INF = -1e30

GLOB_Q = GLOB_HEADS * HEAD_DIM
GLOB_KVW = GLOB_KV * HEAD_DIM
WIN_Q = WIN_HEADS * HEAD_DIM
WIN_KVW = WIN_KV * HEAD_DIM
GROUP = GLOB_HEADS // GLOB_KV
C_U = 0
C_QG = 2 * CONV_CH
C_KG = C_QG + GLOB_Q
C_VG = C_KG + GLOB_KVW
C_QW = C_VG + GLOB_KVW
C_KW = C_QW + WIN_Q
C_VW = C_KW + WIN_KVW
IN_WIDTH = C_VW + WIN_KVW

LOG2E = float(np.log2(np.e))
Q_PRESCALE = HEAD_DIM ** -0.5 * LOG2E

MOD_ROWS = 16
CONV_HALO = 16
SUBLANES = 8
CONV_BLOCK = 256
QSUB = 128
SCORE_LOOKAHEAD = 1
V7X_VMEM_LIMIT = 56 * 1024 * 1024

BF16 = jnp.bfloat16
F32 = jnp.float32


def _params(*sem):
    return pltpu.CompilerParams(dimension_semantics=sem, vmem_limit_bytes=V7X_VMEM_LIMIT)


def _sigmoid(x):
    return 1.0 / (1.0 + jnp.exp2(x * -LOG2E))


def _rms(x, gain):
    return x * lax.rsqrt(jnp.mean(x * x, axis=-1, keepdims=True) + NORM_EPS) * gain


def _ada_kernel(c_ref, w_ref, b_ref, o_ref):
    c = c_ref[...]
    a = (c * _sigmoid(c)).astype(BF16)
    o_ref[...] = jnp.dot(a, w_ref[...].astype(BF16), preferred_element_type=F32) + b_ref[...]


def _ada(cin, w_ada, b_ada):
    depth, d, n = w_ada.shape
    tn = 2048
    return pl.pallas_call(
        _ada_kernel,
        grid=(depth, n // tn),
        in_specs=[
            pl.BlockSpec((MOD_ROWS, d), lambda l, j: (0, 0)),
            pl.BlockSpec((None, d, tn), lambda l, j: (l, 0, j)),
            pl.BlockSpec((None, 1, tn), lambda l, j: (l, 0, j)),
        ],
        out_specs=pl.BlockSpec((None, MOD_ROWS, tn), lambda l, j: (l, 0, j)),
        out_shape=jax.ShapeDtypeStruct((depth, MOD_ROWS, n), F32),
        compiler_params=_params("arbitrary", "arbitrary"),
        name="ada_mod",
    )(cin, w_ada, b_ada.reshape(depth, 1, n))


def _store_cache(ref, kv, cols, t):
    nb, rows, _ = ref.shape
    seq = rows // kv
    heads = range(kv)[slice(cols.start and cols.start // HEAD_DIM, cols.stop and cols.stop // HEAD_DIM)]
    for b in range(nb):
        for i, h in enumerate(heads):
            ref[b, pl.ds(h, seq, stride=kv), :] = t[b * seq:(b + 1) * seq, i * HEAD_DIM:(i + 1) * HEAD_DIM]


def _project(x_ref, mod_ref, g_ref, w_ref, qn_ref, kn_ref, rope_refs, outs, caches):
    u_ref, qg_ref, kg_ref, vg_ref, qw_ref, kw_ref, vw_ref = outs
    h = _rms(x_ref[...], g_ref[...]) * (1.0 + mod_ref[1:2, :]) + mod_ref[0:1, :]
    hb = h.astype(BF16)

    def proj(c0, c1):
        return jnp.dot(hb, w_ref[:, c0:c1], preferred_element_type=F32)

    if rope_refs is not None:
        cos = rope_refs[0][...]
        sin = rope_refs[1][...]
        lane = lax.broadcasted_iota(jnp.int32, cos.shape, 1)
        take_next = ((lane // (HEAD_DIM // 4)) % 2) == 0

        def rot(t):
            r = jnp.where(take_next, pltpu.roll(t, HEAD_DIM - HEAD_DIM // 4, 1),
                          pltpu.roll(t, HEAD_DIM // 4, 1))
            return t * cos + r * sin
    else:
        def rot(t):
            return t

    def heads(p, n):
        return [(slice(i * HEAD_DIM, (i + 1) * HEAD_DIM), p[:, i * HEAD_DIM:(i + 1) * HEAD_DIM])
                for i in range(n)]

    qn = qn_ref[...]
    kn = kn_ref[...]
    for cols, t in heads(proj(C_QG, C_KG), GLOB_HEADS):
        qg_ref[:, cols] = (rot(_rms(t, qn)) * Q_PRESCALE).astype(BF16)
    for cols, t in heads(proj(C_KG, C_VG), GLOB_KV):
        t = _rms(t, kn)
        if caches is not None:
            _store_cache(caches[0], GLOB_KV, cols, t)
        kg_ref[:, cols] = rot(t).astype(BF16)
    p = proj(C_VG, C_QW)
    if caches is not None:
        _store_cache(caches[1], GLOB_KV, slice(None), p)
    vg_ref[...] = p.astype(BF16)
    for cols, t in heads(proj(C_QW, C_KW), WIN_HEADS):
        qw_ref[:, cols] = (rot(t) * Q_PRESCALE).astype(BF16)
    p = proj(C_KW, C_VW)
    if caches is not None:
        _store_cache(caches[2], WIN_KV, slice(None), p)
    for cols, t in heads(p, WIN_KV):
        kw_ref[:, cols] = rot(t).astype(BF16)
    p = proj(C_VW, IN_WIDTH)
    if caches is not None:
        _store_cache(caches[3], WIN_KV, slice(None), p)
    vw_ref[...] = p.astype(BF16)
    u_ref[...] = proj(C_U, C_QG)


PROJ_WIDTHS = ((C_QG - C_U, F32), (GLOB_Q, BF16), (GLOB_KVW, BF16), (GLOB_KVW, BF16),
               (WIN_Q, BF16), (WIN_KVW, BF16), (WIN_KVW, BF16))


def _inproj_kernel(*refs, rope, n_aliased):
    x_ref, mod_ref, g_ref, w_ref, qn_ref, kn_ref = refs[:6]
    pos = 6
    rope_refs = None
    if rope:
        rope_refs = refs[pos:pos + 2]
        pos += 2
    pos += n_aliased
    outs = refs[pos:pos + len(PROJ_WIDTHS)]
    caches = refs[pos + len(PROJ_WIDTHS):] or None
    _project(x_ref, mod_ref, g_ref, w_ref, qn_ref, kn_ref, rope_refs, outs, caches)


def _inproj(x, mod, row_of_tile, g, w, layer, qn, kn, rope_tabs, seq, prev_cache):
    t_tokens, d = x.shape
    tm = 512
    rope = rope_tabs is not None
    const = lambda i: (0, 0)
    in_specs = [
        pl.BlockSpec((tm, d), lambda i: (i, 0)),
        pl.BlockSpec((None, 6, d), lambda i: (row_of_tile(i, tm), 0, 0)),
        pl.BlockSpec((1, d), const),
        pl.BlockSpec((d, IN_WIDTH), const, pipeline_mode=pl.Buffered(1)),
        pl.BlockSpec((1, HEAD_DIM), const),
        pl.BlockSpec((1, HEAD_DIM), const),
    ]
    args = [x, mod, g, w, qn, kn]
    if rope:
        per_seq = seq // tm
        in_specs += [pl.BlockSpec((tm, HEAD_DIM), lambda i: (i % per_seq, 0))] * 2
        args += list(rope_tabs)
    out_specs = [pl.BlockSpec((tm, wd), lambda i: (i, 0)) for wd, _ in PROJ_WIDTHS]
    out_shape = [jax.ShapeDtypeStruct((t_tokens, wd), dt) for wd, dt in PROJ_WIDTHS]
    aliases = {}
    n_aliased = 0
    if prev_cache is not None:
        nb = tm // seq
        n_aliased = len(prev_cache)
        for prev in prev_cache:
            aliases[len(args)] = len(out_shape)
            in_specs.append(pl.BlockSpec(memory_space=pl.ANY))
            args.append(prev)
            out_specs.append(pl.BlockSpec((nb, None) + prev.shape[2:], lambda i: (i, layer, 0, 0)))
            out_shape.append(jax.ShapeDtypeStruct(prev.shape, prev.dtype))
    return pl.pallas_call(
        functools.partial(_inproj_kernel, rope=rope, n_aliased=n_aliased),
        grid=(t_tokens // tm,),
        in_specs=in_specs,
        out_specs=out_specs,
        out_shape=out_shape,
        input_output_aliases=aliases,
        compiler_params=_params("arbitrary"),
        name="inproj_rope" if rope else "inproj_ctx",
    )(*args)


def _conv_body(u_ref, halo, w_ref, b_ref, lg_ref, lb_ref, o_ref, hp_ref, y_ref):
    rows = u_ref.shape[0]

    def glu(u):
        return u[:, :CONV_CH] * _sigmoid(u[:, CONV_CH:])

    zeros = jnp.zeros((CONV_HALO, CONV_CH), F32)
    if halo is not None:
        up_ref, un_ref, has_prev, has_next = halo
        hp_ref[0:CONV_HALO, :] = jnp.where(has_prev, glu(up_ref[...]), zeros)
        hp_ref[CONV_HALO + rows:, :] = jnp.where(has_next, glu(un_ref[...]), zeros)
    else:
        hp_ref[0:CONV_HALO, :] = zeros
        hp_ref[CONV_HALO + rows:, :] = zeros
    hp_ref[CONV_HALO:CONV_HALO + rows, :] = glu(u_ref[...])

    rchunk = 128
    base = CONV_HALO - CONV_WIDTH // 2
    groups = -(-(base + CONV_WIDTH) // SUBLANES)
    for c in range(CONV_CH // HEAD_DIM):
        cs = slice(c * HEAD_DIM, (c + 1) * HEAD_DIM)
        for r in range(rows // rchunk):
            acc = None
            for s in range(SUBLANES):
                part = None
                for q in range(groups):
                    k = q * SUBLANES + s - base
                    if 0 <= k < CONV_WIDTH:
                        r0 = r * rchunk + q * SUBLANES
                        term = hp_ref[r0:r0 + rchunk + SUBLANES, cs] * w_ref[k:k + 1, cs]
                        part = term if part is None else part + term
                part = part[s:s + rchunk]
                acc = part if acc is None else acc + part
            y_ref[r * rchunk:(r + 1) * rchunk, cs] = acc

    y = y_ref[...] + b_ref[...]
    mu = jnp.mean(y, axis=-1, keepdims=True)
    yc = y - mu
    var = jnp.mean(yc * yc, axis=-1, keepdims=True)
    z = yc * lax.rsqrt(var + NORM_EPS) * lg_ref[...] + lb_ref[...]
    o_ref[...] = (z * _sigmoid(z)).astype(BF16)


def _conv_kernel(*refs, blocks_per_seq):
    if blocks_per_seq > 1:
        u_ref, up_ref, un_ref = refs[:3]
        j = pl.program_id(0) % blocks_per_seq
        halo = (up_ref, un_ref, j > 0, j < blocks_per_seq - 1)
        rest = refs[3:]
    else:
        u_ref, halo, rest = refs[0], None, refs[1:]
    _conv_body(u_ref, halo, *rest)


def _conv(u, w, b, lg, lb, seq):
    t_tokens = u.shape[0]
    rows = CONV_BLOCK
    blocks_per_seq = seq // rows
    halo_per_block = rows // CONV_HALO
    n_halo = t_tokens // CONV_HALO
    const = lambda i: (0, 0)
    in_specs = [pl.BlockSpec((rows, 2 * CONV_CH), lambda i: (i, 0))]
    args = [u]
    if blocks_per_seq > 1:
        in_specs += [
            pl.BlockSpec((CONV_HALO, 2 * CONV_CH), lambda i: (jnp.maximum(i * halo_per_block - 1, 0), 0)),
            pl.BlockSpec((CONV_HALO, 2 * CONV_CH),
                         lambda i: (jnp.minimum((i + 1) * halo_per_block, n_halo - 1), 0)),
        ]
        args += [u, u]
    in_specs += [pl.BlockSpec((CONV_WIDTH, CONV_CH), const)] + [pl.BlockSpec((1, CONV_CH), const)] * 3
    args += [w, b, lg, lb]
    return pl.pallas_call(
        functools.partial(_conv_kernel, blocks_per_seq=blocks_per_seq),
        grid=(t_tokens // rows,),
        in_specs=in_specs,
        out_specs=pl.BlockSpec((rows, CONV_CH), lambda i: (i, 0)),
        out_shape=jax.ShapeDtypeStruct((t_tokens, CONV_CH), BF16),
        scratch_shapes=[pltpu.VMEM((rows + 2 * CONV_HALO, CONV_CH), F32),
                        pltpu.VMEM((rows, CONV_CH), F32)],
        compiler_params=_params("arbitrary"),
        name="conv_mixer",
    )(*args)


def _stack_heads(q, g):
    return jnp.concatenate(
        [q[:, (g * GROUP + j) * HEAD_DIM:(g * GROUP + j + 1) * HEAD_DIM] for j in range(GROUP)], axis=0)


def _scores(q, k):
    return lax.dot_general(q, k, (((1,), (1,)), ((), ())), preferred_element_type=F32)


def _sink_column(sink_ref, g, rows):
    return jnp.concatenate(
        [jnp.full((rows, 1), sink_ref[g * GROUP + j] * LOG2E, F32) for j in range(GROUP)], axis=0)


def _store_heads(o_ref, o, g, rows):
    for j in range(GROUP):
        c0 = (g * GROUP + j) * HEAD_DIM
        o_ref[:, c0:c0 + HEAD_DIM] = o[j * rows:(j + 1) * rows].astype(o_ref.dtype)


def _attention_jobs(jobs):
    n = len(jobs)
    pieces = [None] * n
    for t in range(-SCORE_LOOKAHEAD, n):
        if t + SCORE_LOOKAHEAD < n:
            pieces[t + SCORE_LOOKAHEAD] = jobs[t + SCORE_LOOKAHEAD][0]()
        if t < 0:
            continue
        _, snk, store = jobs[t]
        m = functools.reduce(jnp.maximum, [jnp.max(s, axis=-1, keepdims=True) for s, _ in pieces[t]])
        if snk is not None:
            m = jnp.maximum(m, snk)
        es = [jnp.exp2(s - m) for s, _ in pieces[t]]
        den = functools.reduce(jnp.add, [jnp.sum(e, axis=-1, keepdims=True) for e in es])
        if snk is not None:
            den = den + jnp.exp2(snk - m)
        o = functools.reduce(jnp.add, [jnp.dot(e.astype(BF16), v, preferred_element_type=F32)
                                       for e, (_, v) in zip(es, pieces[t])])
        store(o * (1.0 / den))
        pieces[t] = None


def _attn_ctx_kernel(sink_ref, qg_ref, kg_ref, vg_ref, qw_ref, kw_ref, vw_ref, og_ref, ow_ref, *, seq):
    jobs = []
    for b in range(qg_ref.shape[0] // seq):
        rs = slice(b * seq, (b + 1) * seq)
        for g in range(GLOB_KV):
            hs = slice(g * HEAD_DIM, (g + 1) * HEAD_DIM)

            def glob_scores(rs=rs, hs=hs, g=g):
                return [(_scores(_stack_heads(qg_ref[rs, :], g), kg_ref[rs, hs]), vg_ref[rs, hs])]

            def win_scores(rs=rs, hs=hs, g=g):
                return [(_scores(_stack_heads(qw_ref[rs, :], g), kw_ref[rs, hs]), vw_ref[rs, hs])]

            jobs.append((glob_scores, None, functools.partial(_store_heads, og_ref.at[rs, :], g=g, rows=seq)))
            jobs.append((win_scores, _sink_column(sink_ref, g, seq),
                         functools.partial(_store_heads, ow_ref.at[rs, :], g=g, rows=seq)))
    _attention_jobs(jobs)


def _attn_ctx(sink, qg, kg, vg, qw, kw, vw, seq):
    t_tokens = qg.shape[0]
    rows = 8 * seq
    spec = lambda wd: pl.BlockSpec((rows, wd), lambda i: (i, 0))
    return pl.pallas_call(
        functools.partial(_attn_ctx_kernel, seq=seq),
        grid=(t_tokens // rows,),
        in_specs=[pl.BlockSpec(memory_space=pltpu.SMEM),
                  spec(GLOB_Q), spec(GLOB_KVW), spec(GLOB_KVW), spec(WIN_Q), spec(WIN_KVW), spec(WIN_KVW)],
        out_specs=[spec(GLOB_Q), spec(WIN_Q)],
        out_shape=[jax.ShapeDtypeStruct((t_tokens, GLOB_Q), BF16),
                   jax.ShapeDtypeStruct((t_tokens, WIN_Q), BF16)],
        compiler_params=_params("arbitrary"),
        name="attn_ctx",
    )(sink, qg, kg, vg, qw, kw, vw)


def _attn_lat_kernel(sink_ref, qg_ref, kg_ref, vg_ref, cgk_ref, cgv_ref,
                     qw_ref, kw_ref, vw_ref, cwk_ref, cwv_ref, og_ref, ow_ref,
                     cgk_b, cgv_b, cwk_b, cwv_b, *, seq, qb):
    bi = pl.program_id(1)

    @pl.when(bi == 0)
    def _():
        for src, dst in ((cgk_ref, cgk_b), (cgv_ref, cgv_b), (cwk_ref, cwk_b), (cwv_ref, cwv_b)):
            past = dst.shape[0]
            kv = src.shape[0] // past
            for h in range(kv):
                dst[:, h * HEAD_DIM:(h + 1) * HEAD_DIM] = src[pl.ds(h, past, stride=kv), :].astype(BF16)

    span = QSUB + 2 * WINDOW
    jobs = []
    for sb in range(qb // QSUB):
        rs = slice(sb * QSUB, (sb + 1) * QSUB)
        q0 = bi * qb + sb * QSUB
        start = pl.multiple_of(jnp.clip(q0 - WINDOW, 0, seq - span), HEAD_DIM)
        qpos = q0 + lax.broadcasted_iota(jnp.int32, (QSUB, span), 0)
        kpos = start + lax.broadcasted_iota(jnp.int32, (QSUB, span), 1)
        band = jnp.where(jnp.abs(qpos - kpos) <= WINDOW, 0.0, NEG_INF)
        band = jnp.concatenate([band] * GROUP, axis=0)
        for g in range(GLOB_KV):
            hs = slice(g * HEAD_DIM, (g + 1) * HEAD_DIM)

            def glob_scores(rs=rs, hs=hs, g=g):
                q = _stack_heads(qg_ref[rs, :], g)
                return [(_scores(q, cgk_b[:, hs]), cgv_b[:, hs]), (_scores(q, kg_ref[:, hs]), vg_ref[:, hs])]

            def win_scores(rs=rs, hs=hs, g=g, start=start, band=band):
                q = _stack_heads(qw_ref[rs, :], g)
                s_loc = _scores(q, kw_ref[pl.ds(start, span), hs]) + band
                return [(_scores(q, cwk_b[:, hs]), cwv_b[:, hs]), (s_loc, vw_ref[pl.ds(start, span), hs])]

            jobs.append((glob_scores, None, functools.partial(_store_heads, og_ref.at[rs, :], g=g, rows=QSUB)))
            jobs.append((win_scores, _sink_column(sink_ref, g, QSUB),
                         functools.partial(_store_heads, ow_ref.at[rs, :], g=g, rows=QSUB)))
    _attention_jobs(jobs[0::2] + jobs[1::2])


def _attn_lat(sink, layer, qg, kg, vg, cgk, cgv, qw, kw, vw, cwk, cwv, seq):
    t_tokens = qg.shape[0]
    nbatch = t_tokens // seq
    qb = QSUB
    nqb = seq // qb
    past = cgk.shape[2] // GLOB_KV
    qspec = lambda wd: pl.BlockSpec((qb, wd), lambda b, i: (b * nqb + i, 0))
    kvspec = lambda wd: pl.BlockSpec((seq, wd), lambda b, i: (b, 0))
    cspec = lambda wd: pl.BlockSpec((None, None, past * (wd // HEAD_DIM), HEAD_DIM), lambda b, i: (b, layer, 0, 0))
    return pl.pallas_call(
        functools.partial(_attn_lat_kernel, seq=seq, qb=qb),
        grid=(nbatch, nqb),
        in_specs=[pl.BlockSpec(memory_space=pltpu.SMEM),
                  qspec(GLOB_Q), kvspec(GLOB_KVW), kvspec(GLOB_KVW), cspec(GLOB_KVW), cspec(GLOB_KVW),
                  qspec(WIN_Q), kvspec(WIN_KVW), kvspec(WIN_KVW), cspec(WIN_KVW), cspec(WIN_KVW)],
        out_specs=[qspec(GLOB_Q), qspec(WIN_Q)],
        out_shape=[jax.ShapeDtypeStruct((t_tokens, GLOB_Q), BF16),
                   jax.ShapeDtypeStruct((t_tokens, WIN_Q), BF16)],
        scratch_shapes=[pltpu.VMEM((past, wd), BF16) for wd in (GLOB_KVW, GLOB_KVW, WIN_KVW, WIN_KVW)],
        compiler_params=_params("arbitrary", "arbitrary"),
        name="attn_lat",
    )(sink, qg, kg, vg, cgk, cgv, qw, kw, vw, cwk, cwv)


def _outproj_kernel(x_ref, mod_ref, g_ref, yc_ref, yg_ref, yw_ref, w_ref, o_ref, h_ref, *, rchunk):
    c1 = CONV_CH
    c2 = CONV_CH + GLOB_Q
    for r in range(x_ref.shape[0] // rchunk):
        rs = slice(r * rchunk, (r + 1) * rchunk)
        mix = (jnp.dot(yc_ref[rs, :], w_ref[0:c1, :], preferred_element_type=F32)
               + jnp.dot(yg_ref[rs, :], w_ref[c1:c2, :], preferred_element_type=F32)
               + jnp.dot(yw_ref[rs, :], w_ref[c2:, :], preferred_element_type=F32))
        x1 = x_ref[rs, :] + mod_ref[2:3, :] * mix
        o_ref[rs, :] = x1
        h = _rms(x1, g_ref[...]) * (1.0 + mod_ref[4:5, :]) + mod_ref[3:4, :]
        h_ref[rs, :] = h.astype(BF16)


def _outproj(x, mod, row_of_tile, g_mlp, yc, yg, yw, w):
    t_tokens, d = x.shape
    tm = 512
    row = lambda wd: pl.BlockSpec((tm, wd), lambda i: (i, 0))
    return pl.pallas_call(
        functools.partial(_outproj_kernel, rchunk=256),
        grid=(t_tokens // tm,),
        in_specs=[row(d),
                  pl.BlockSpec((None, 6, d), lambda i: (row_of_tile(i, tm), 0, 0)),
                  pl.BlockSpec((1, d), lambda i: (0, 0)),
                  row(CONV_CH), row(GLOB_Q), row(WIN_Q),
                  pl.BlockSpec(w.shape, lambda i: (0, 0), pipeline_mode=pl.Buffered(1))],
        out_specs=[row(d), row(d)],
        out_shape=[jax.ShapeDtypeStruct((t_tokens, d), F32), jax.ShapeDtypeStruct((t_tokens, d), BF16)],
        compiler_params=_params("arbitrary"),
        name="outproj",
    )(x, mod, g_mlp, yc, yg, yw, w)


def _mlp_kernel(*refs, final_norm, n_cast):
    n_in = 5 + int(final_norm)
    x_ref, h_ref, mod_ref, w1_ref, w2_ref = refs[:5]
    cast_in = refs[n_in:n_in + n_cast]
    o_ref = refs[n_in + n_cast]
    cast_out = refs[n_in + n_cast + 1:]
    acc_ref = o_ref
    j = pl.program_id(1)
    last = pl.num_programs(1) - 1

    def partial_sum():
        a = jnp.maximum(jnp.dot(h_ref[...], w1_ref[...], preferred_element_type=F32), 0.0)
        part = jnp.dot((a * a).astype(BF16), w2_ref[...], preferred_element_type=F32)
        for src, dst in zip(cast_in, cast_out):
            dst[...] = src[...].astype(BF16)
        return part

    @pl.when(j == 0)
    def _():
        acc_ref[...] = partial_sum()

    @pl.when(jnp.logical_and(j > 0, j < last))
    def _():
        acc_ref[...] += partial_sum()

    @pl.when(j == last)
    def _():
        y = x_ref[...] + mod_ref[5:6, :] * (acc_ref[...] + partial_sum())
        if final_norm:
            y = _rms(y, refs[5][...])
        o_ref[...] = y


def _mlp(x, h, mod, row_of_tile, w1, w2, g_final, cast_jobs):
    t_tokens, d = x.shape
    d_ff = w1.shape[1]
    tm, tf = 512, 1024
    ni, nj = t_tokens // tm, d_ff // tf
    assert nj >= 2
    final_norm = g_final is not None
    in_specs = [
        pl.BlockSpec((tm, d), lambda i, j: (i, 0)),
        pl.BlockSpec((tm, d), lambda i, j: (i, 0)),
        pl.BlockSpec((None, 6, d), lambda i, j: (row_of_tile(i, tm), 0, 0)),
        pl.BlockSpec((d, tf), lambda i, j: (0, j)),
        pl.BlockSpec((tf, d), lambda i, j: (j, 0)),
    ]
    args = [x, h, mod, w1, w2]
    if final_norm:
        in_specs.append(pl.BlockSpec((1, d), lambda i, j: (0, 0)))
        args.append(g_final)
    out_specs = [pl.BlockSpec((tm, d), lambda i, j: (i, 0))]
    out_shape = [jax.ShapeDtypeStruct((t_tokens, d), F32)]
    for w, layer in cast_jobs:
        _, rows, cols = w.shape
        rps = rows // (ni * nj)
        assert rps * ni * nj == rows and rps % 16 == 0
        in_specs.append(pl.BlockSpec((None, rps, cols), lambda i, j, layer=layer: (layer, i * nj + j, 0)))
        args.append(w)
        out_specs.append(pl.BlockSpec((rps, cols), lambda i, j: (i * nj + j, 0)))
        out_shape.append(jax.ShapeDtypeStruct((rows, cols), BF16))
    return pl.pallas_call(
        functools.partial(_mlp_kernel, final_norm=final_norm, n_cast=len(cast_jobs)),
        grid=(ni, nj),
        in_specs=in_specs,
        out_specs=out_specs,
        out_shape=out_shape,
        compiler_params=_params("arbitrary", "arbitrary"),
        name="mlp_final" if final_norm else "mlp",
    )(*args)


def _rope_tables(n_tokens):
    rows = n_tokens // GRID_W
    r, col = jnp.meshgrid(jnp.arange(rows), jnp.arange(GRID_W), indexing='ij')
    r = r.reshape(-1).astype(F32)
    col = col.reshape(-1).astype(F32)
    n_freq = HEAD_DIM // 4
    inv = ROPE_THETA ** (-jnp.arange(n_freq, dtype=F32) / n_freq)
    ang_r = r[:, None] * inv
    ang_c = col[:, None] * inv
    ang = jnp.concatenate([ang_r, ang_r, ang_c, ang_c], axis=-1)
    sign = jnp.asarray(np.tile(np.repeat(np.array([-1.0, 1.0], np.float32), n_freq), 2))
    return jnp.cos(ang), jnp.sin(ang) * sign


def kernel(x_prompt, x_sample, cache_glob_k, cache_glob_v, cache_win_k, cache_win_v, c, c_ctx, w_ada, b_ada,
           g_attn, g_mlp, w_in, conv_w, conv_b, conv_ln_g, conv_ln_b, q_norm_g, k_norm_g, sink, w_out,
           w_mlp1, w_mlp2, g_final):
    batch, seq, d = x_prompt.shape
    dec_batch, dec_seq, _ = x_sample.shape
    depth = w_ada.shape[0]
    ctx_row = dec_batch
    assert dec_batch < MOD_ROWS

    cin = jnp.concatenate([c, c_ctx[None, :], jnp.zeros((MOD_ROWS - dec_batch - 1, d), F32)], axis=0)
    mods = _ada(cin, w_ada, b_ada).reshape(depth, MOD_ROWS, 6, d)
    rope_tabs = _rope_tables(dec_seq)

    w_in_b, w_out_b, w1_b, w2_b = (w[0].astype(BF16) for w in (w_in, w_out, w_mlp1, w_mlp2))
    caches = [a.reshape(dec_batch, depth, -1, HEAD_DIM) for a in (cache_glob_k, cache_glob_v, cache_win_k, cache_win_v)]

    ctx_rows = lambda i, tm: ctx_row
    lat_rows = lambda i, tm: (i * tm) // dec_seq

    xp = x_prompt.reshape(batch * seq, d)
    xs = x_sample.reshape(dec_batch * dec_seq, d)
    new_cache = [jnp.zeros((batch, depth, seq * kv, HEAD_DIM), F32) for kv in (GLOB_KV, GLOB_KV, WIN_KV, WIN_KV)]
    for l in range(depth):
        mod = mods[l]
        row = lambda a: a[l].reshape(1, -1)
        g_fin = g_final.reshape(1, d) if l == depth - 1 else None
        conv_args = (conv_w[l], row(conv_b), row(conv_ln_g), row(conv_ln_b))
        more = l + 1 < depth
        cast_ctx = [(w_in, l + 1), (w_out, l + 1), (w_mlp1, l + 1)] if more else []
        cast_lat = [(w_mlp2, l + 1)] if more else []

        u, qg, kg, vg, qw, kw, vw, *new_cache = _inproj(
            xp, mod, ctx_rows, row(g_attn), w_in_b, l, row(q_norm_g), row(k_norm_g), None, seq, new_cache)
        yc = _conv(u, *conv_args, seq)
        yg, yw = _attn_ctx(sink[l], qg, kg, vg, qw, kw, vw, seq)
        xp, hp = _outproj(xp, mod, ctx_rows, row(g_mlp), yc, yg, yw, w_out_b)
        xp, *next_ctx = _mlp(xp, hp, mod, ctx_rows, w1_b, w2_b, g_fin, cast_ctx)

        u, qg, kg, vg, qw, kw, vw = _inproj(
            xs, mod, lat_rows, row(g_attn), w_in_b, l, row(q_norm_g), row(k_norm_g), rope_tabs, dec_seq, None)
        yc = _conv(u, *conv_args, dec_seq)
        yg, yw = _attn_lat(sink[l], l, qg, kg, vg, caches[0], caches[1], qw, kw, vw, caches[2], caches[3],
                           dec_seq)
        xs, hs = _outproj(xs, mod, lat_rows, row(g_mlp), yc, yg, yw, w_out_b)
        xs, *next_lat = _mlp(xs, hs, mod, lat_rows, w1_b, w2_b, g_fin, cast_lat)
        if more:
            (w_in_b, w_out_b, w1_b), (w2_b,) = next_ctx, next_lat

    outs = [a.reshape(batch, depth, seq, -1, HEAD_DIM) for a in new_cache]
    return (xp.reshape(batch, seq, d), xs.reshape(dec_batch, dec_seq, d), *outs)
```

```python
import functools

import jax
import jax.numpy as jnp
import numpy as np
from jax import lax
from jax.experimental import pallas as pl
from jax.experimental.pallas import tpu as pltpu

HEAD_DIM = 128
CONV_CH = 512
CONV_WIDTH = 31
GLOB_HEADS = 6
GLOB_KV = 2
WIN_HEADS = 6
WIN_KV = 2
WINDOW = 128
GRID_W = 64
ROPE_THETA = 10000.0
NORM_EPS = 1e-6
NEG_INF = -1e30

GLOB_Q = GLOB_HEADS * HEAD_DIM
GLOB_KVW = GLOB_KV * HEAD_DIM
WIN_Q = WIN_HEADS * HEAD_DIM
WIN_KVW = WIN_KV * HEAD_DIM
GROUP = GLOB_HEADS // GLOB_KV
C_U = 0
C_QG = 2 * CONV_CH
C_KG = C_QG + GLOB_Q
C_VG = C_KG + GLOB_KVW
C_QW = C_VG + GLOB_KVW
C_KW = C_QW + WIN_Q
C_VW = C_KW + WIN_KVW
IN_WIDTH = C_VW + WIN_KVW

LOG2E = float(np.log2(np.e))
Q_PRESCALE = HEAD_DIM ** -0.5 * LOG2E

MOD_ROWS = 16
CONV_HALO = 16
SUBLANES = 8
CONV_BLOCK = 256
QSUB = 128
SCORE_LOOKAHEAD = 1
V7X_VMEM_LIMIT = 56 * 1024 * 1024

BF16 = jnp.bfloat16
F32 = jnp.float32


def _params(*sem):
    return pltpu.CompilerParams(dimension_semantics=sem, vmem_limit_bytes=V7X_VMEM_LIMIT)


def _sigmoid(x):
    return 1.0 / (1.0 + jnp.exp2(x * -LOG2E))


def _rms(x, gain):
    return x * lax.rsqrt(jnp.mean(x * x, axis=-1, keepdims=True) + NORM_EPS) * gain


def _ada_kernel(c_ref, w_ref, b_ref, o_ref):
    c = c_ref[...]
    a = (c * _sigmoid(c)).astype(BF16)
    o_ref[...] = jnp.dot(a, w_ref[...].astype(BF16), preferred_element_type=F32) + b_ref[...]


def _ada(cin, w_ada, b_ada):
    depth, d, n = w_ada.shape
    tn = 2048
    return pl.pallas_call(
        _ada_kernel,
        grid=(depth, n // tn),
        in_specs=[
            pl.BlockSpec((MOD_ROWS, d), lambda l, j: (0, 0)),
            pl.BlockSpec((None, d, tn), lambda l, j: (l, 0, j)),
            pl.BlockSpec((None, 1, tn), lambda l, j: (l, 0, j)),
        ],
        out_specs=pl.BlockSpec((None, MOD_ROWS, tn), lambda l, j: (l, 0, j)),
        out_shape=jax.ShapeDtypeStruct((depth, MOD_ROWS, n), F32),
        compiler_params=_params("arbitrary", "arbitrary"),
        name="ada_mod",
    )(cin, w_ada, b_ada.reshape(depth, 1, n))


def _store_cache(ref, kv, cols, t):
    nb, rows, _ = ref.shape
    seq = rows // kv
    heads = range(kv)[slice(cols.start and cols.start // HEAD_DIM, cols.stop and cols.stop // HEAD_DIM)]
    for b in range(nb):
        for i, h in enumerate(heads):
            ref[b, pl.ds(h, seq, stride=kv), :] = t[b * seq:(b + 1) * seq, i * HEAD_DIM:(i + 1) * HEAD_DIM]


def _project(x_ref, mod_ref, g_ref, w_ref, qn_ref, kn_ref, rope_refs, outs, caches):
    u_ref, qg_ref, kg_ref, vg_ref, qw_ref, kw_ref, vw_ref = outs
    h = _rms(x_ref[...], g_ref[...]) * (1.0 + mod_ref[1:2, :]) + mod_ref[0:1, :]
    hb = h.astype(BF16)

    def proj(c0, c1):
        return jnp.dot(hb, w_ref[:, c0:c1], preferred_element_type=F32)

    if rope_refs is not None:
        cos = rope_refs[0][...]
        sin = rope_refs[1][...]
        lane = lax.broadcasted_iota(jnp.int32, cos.shape, 1)
        take_next = ((lane // (HEAD_DIM // 4)) % 2) == 0

        def rot(t):
            r = jnp.where(take_next, pltpu.roll(t, HEAD_DIM - HEAD_DIM // 4, 1),
                          pltpu.roll(t, HEAD_DIM // 4, 1))
            return t * cos + r * sin
    else:
        def rot(t):
            return t

    def heads(p, n):
        return [(slice(i * HEAD_DIM, (i + 1) * HEAD_DIM), p[:, i * HEAD_DIM:(i + 1) * HEAD_DIM])
                for i in range(n)]

    qn = qn_ref[...]
    kn = kn_ref[...]
    for cols, t in heads(proj(C_QG, C_KG), GLOB_HEADS):
        qg_ref[:, cols] = (rot(_rms(t, qn)) * Q_PRESCALE).astype(BF16)
    for cols, t in heads(proj(C_KG, C_VG), GLOB_KV):
        t = _rms(t, kn)
        if caches is not None:
            _store_cache(caches[0], GLOB_KV, cols, t)
        kg_ref[:, cols] = rot(t).astype(BF16)
    p = proj(C_VG, C_QW)
    if caches is not None:
        _store_cache(caches[1], GLOB_KV, slice(None), p)
    vg_ref[...] = p.astype(BF16)
    for cols, t in heads(proj(C_QW, C_KW), WIN_HEADS):
        qw_ref[:, cols] = (rot(t) * Q_PRESCALE).astype(BF16)
    p = proj(C_KW, C_VW)
    if caches is not None:
        _store_cache(caches[2], WIN_KV, slice(None), p)
    for cols, t in heads(p, WIN_KV):
        kw_ref[:, cols] = rot(t).astype(BF16)
    p = proj(C_VW, IN_WIDTH)
    if caches is not None:
        _store_cache(caches[3], WIN_KV, slice(None), p)
    vw_ref[...] = p.astype(BF16)
    u_ref[...] = proj(C_U, C_QG)


PROJ_WIDTHS = ((C_QG - C_U, F32), (GLOB_Q, BF16), (GLOB_KVW, BF16), (GLOB_KVW, BF16),
               (WIN_Q, BF16), (WIN_KVW, BF16), (WIN_KVW, BF16))


def _inproj_kernel(*refs, rope, n_aliased):
    x_ref, mod_ref, g_ref, w_ref, qn_ref, kn_ref = refs[:6]
    pos = 6
    rope_refs = None
    if rope:
        rope_refs = refs[pos:pos + 2]
        pos += 2
    pos += n_aliased
    outs = refs[pos:pos + len(PROJ_WIDTHS)]
    caches = refs[pos + len(PROJ_WIDTHS):] or None
    _project(x_ref, mod_ref, g_ref, w_ref, qn_ref, kn_ref, rope_refs, outs, caches)


def _inproj(x, mod, row_of_tile, g, w, layer, qn, kn, rope_tabs, seq, prev_cache):
    t_tokens, d = x.shape
    tm = 512
    rope = rope_tabs is not None
    const = lambda i: (0, 0)
    in_specs = [
        pl.BlockSpec((tm, d), lambda i: (i, 0)),
        pl.BlockSpec((None, 6, d), lambda i: (row_of_tile(i, tm), 0, 0)),
        pl.BlockSpec((1, d), const),
        pl.BlockSpec((d, IN_WIDTH), const, pipeline_mode=pl.Buffered(1)),
        pl.BlockSpec((1, HEAD_DIM), const),
        pl.BlockSpec((1, HEAD_DIM), const),
    ]
    args = [x, mod, g, w, qn, kn]
    if rope:
        per_seq = seq // tm
        in_specs += [pl.BlockSpec((tm, HEAD_DIM), lambda i: (i % per_seq, 0))] * 2
        args += list(rope_tabs)
    out_specs = [pl.BlockSpec((tm, wd), lambda i: (i, 0)) for wd, _ in PROJ_WIDTHS]
    out_shape = [jax.ShapeDtypeStruct((t_tokens, wd), dt) for wd, dt in PROJ_WIDTHS]
    aliases = {}
    n_aliased = 0
    if prev_cache is not None:
        nb = tm // seq
        n_aliased = len(prev_cache)
        for prev in prev_cache:
            aliases[len(args)] = len(out_shape)
            in_specs.append(pl.BlockSpec(memory_space=pl.ANY))
            args.append(prev)
            out_specs.append(pl.BlockSpec((nb, None) + prev.shape[2:], lambda i: (i, layer, 0, 0)))
            out_shape.append(jax.ShapeDtypeStruct(prev.shape, prev.dtype))
    return pl.pallas_call(
        functools.partial(_inproj_kernel, rope=rope, n_aliased=n_aliased),
        grid=(t_tokens // tm,),
        in_specs=in_specs,
        out_specs=out_specs,
        out_shape=out_shape,
        input_output_aliases=aliases,
        compiler_params=_params("arbitrary"),
        name="inproj_rope" if rope else "inproj_ctx",
    )(*args)


def _conv_body(u_ref, halo, w_ref, b_ref, lg_ref, lb_ref, o_ref, hp_ref):
    rows = u_ref.shape[0]

    def glu(u):
        return u[:, :CONV_CH] * _sigmoid(u[:, CONV_CH:])

    zeros = jnp.zeros((CONV_HALO, CONV_CH), F32)
    if halo is not None:
        up_ref, un_ref, has_prev, has_next = halo
        hp_ref[0:CONV_HALO, :] = jnp.where(has_prev, glu(up_ref[...]), zeros)
        hp_ref[CONV_HALO + rows:, :] = jnp.where(has_next, glu(un_ref[...]), zeros)
    else:
        hp_ref[0:CONV_HALO, :] = zeros
        hp_ref[CONV_HALO + rows:, :] = zeros
    hp_ref[CONV_HALO:CONV_HALO + rows, :] = glu(u_ref[...])

    rchunk = 128
    base = CONV_HALO - CONV_WIDTH // 2
    groups = -(-(base + CONV_WIDTH) // SUBLANES)
    cols = []
    for c in range(CONV_CH // HEAD_DIM):
        cs = slice(c * HEAD_DIM, (c + 1) * HEAD_DIM)
        col = []
        for r in range(rows // rchunk):
            acc = None
            for s in range(SUBLANES):
                part = None
                for q in range(groups):
                    k = q * SUBLANES + s - base
                    if 0 <= k < CONV_WIDTH:
                        r0 = r * rchunk + q * SUBLANES
                        term = hp_ref[r0:r0 + rchunk + SUBLANES, cs] * w_ref[k:k + 1, cs]
                        part = term if part is None else part + term
                part = part[s:s + rchunk]
                acc = part if acc is None else acc + part
            col.append(acc)
        cols.append(jnp.concatenate(col, axis=0))

    y = jnp.concatenate(cols, axis=1) + b_ref[...]
    mu = jnp.mean(y, axis=-1, keepdims=True)
    yc = y - mu
    var = jnp.mean(yc * yc, axis=-1, keepdims=True)
    z = yc * lax.rsqrt(var + NORM_EPS) * lg_ref[...] + lb_ref[...]
    o_ref[...] = (z * _sigmoid(z)).astype(BF16)


def _conv_kernel(*refs, blocks_per_seq):
    if blocks_per_seq > 1:
        u_ref, up_ref, un_ref = refs[:3]
        j = pl.program_id(0) % blocks_per_seq
        halo = (up_ref, un_ref, j > 0, j < blocks_per_seq - 1)
        rest = refs[3:]
    else:
        u_ref, halo, rest = refs[0], None, refs[1:]
    _conv_body(u_ref, halo, *rest)


def _conv(u, w, b, lg, lb, seq):
    t_tokens = u.shape[0]
    rows = CONV_BLOCK
    blocks_per_seq = seq // rows
    halo_per_block = rows // CONV_HALO
    n_halo = t_tokens // CONV_HALO
    const = lambda i: (0, 0)
    in_specs = [pl.BlockSpec((rows, 2 * CONV_CH), lambda i: (i, 0))]
    args = [u]
    if blocks_per_seq > 1:
        in_specs += [
            pl.BlockSpec((CONV_HALO, 2 * CONV_CH), lambda i: (jnp.maximum(i * halo_per_block - 1, 0), 0)),
            pl.BlockSpec((CONV_HALO, 2 * CONV_CH),
                         lambda i: (jnp.minimum((i + 1) * halo_per_block, n_halo - 1), 0)),
        ]
        args += [u, u]
    in_specs += [pl.BlockSpec((CONV_WIDTH, CONV_CH), const)] + [pl.BlockSpec((1, CONV_CH), const)] * 3
    args += [w, b, lg, lb]
    return pl.pallas_call(
        functools.partial(_conv_kernel, blocks_per_seq=blocks_per_seq),
        grid=(t_tokens // rows,),
        in_specs=in_specs,
        out_specs=pl.BlockSpec((rows, CONV_CH), lambda i: (i, 0)),
        out_shape=jax.ShapeDtypeStruct((t_tokens, CONV_CH), BF16),
        scratch_shapes=[pltpu.VMEM((rows + 2 * CONV_HALO, CONV_CH), F32)],
        compiler_params=_params("arbitrary"),
        name="conv_mixer",
    )(*args)


def _stack_heads(q, g):
    return jnp.concatenate(
        [q[:, (g * GROUP + j) * HEAD_DIM:(g * GROUP + j + 1) * HEAD_DIM] for j in range(GROUP)], axis=0)


def _scores(q, k):
    return lax.dot_general(q, k, (((1,), (1,)), ((), ())), preferred_element_type=F32)


def _sink_column(sink_ref, g, rows):
    return jnp.concatenate(
        [jnp.full((rows, 1), sink_ref[g * GROUP + j] * LOG2E, F32) for j in range(GROUP)], axis=0)


def _store_heads(o_ref, o, g, rows):
    for j in range(GROUP):
        c0 = (g * GROUP + j) * HEAD_DIM
        o_ref[:, c0:c0 + HEAD_DIM] = o[j * rows:(j + 1) * rows].astype(o_ref.dtype)


def _attention_jobs(jobs):
    n = len(jobs)
    pieces = [None] * n
    for t in range(-SCORE_LOOKAHEAD, n):
        if t + SCORE_LOOKAHEAD < n:
            pieces[t + SCORE_LOOKAHEAD] = jobs[t + SCORE_LOOKAHEAD][0]()
        if t < 0:
            continue
        _, snk, store = jobs[t]
        m = functools.reduce(jnp.maximum, [jnp.max(s, axis=-1, keepdims=True) for s, _ in pieces[t]])
        if snk is not None:
            m = jnp.maximum(m, snk)
        es = [jnp.exp2(s - m) for s, _ in pieces[t]]
        den = functools.reduce(jnp.add, [jnp.sum(e, axis=-1, keepdims=True) for e in es])
        if snk is not None:
            den = den + jnp.exp2(snk - m)
        o = functools.reduce(jnp.add, [jnp.dot(e.astype(BF16), v, preferred_element_type=F32)
                                       for e, (_, v) in zip(es, pieces[t])])
        store(o * (1.0 / den))
        pieces[t] = None


def _attn_ctx_kernel(sink_ref, qg_ref, kg_ref, vg_ref, qw_ref, kw_ref, vw_ref, og_ref, ow_ref, *, seq):
    jobs = []
    for b in range(qg_ref.shape[0] // seq):
        rs = slice(b * seq, (b + 1) * seq)
        for g in range(GLOB_KV):
            hs = slice(g * HEAD_DIM, (g + 1) * HEAD_DIM)

            def glob_scores(rs=rs, hs=hs, g=g):
                return [(_scores(_stack_heads(qg_ref[rs, :], g), kg_ref[rs, hs]), vg_ref[rs, hs])]

            def win_scores(rs=rs, hs=hs, g=g):
                return [(_scores(_stack_heads(qw_ref[rs, :], g), kw_ref[rs, hs]), vw_ref[rs, hs])]

            jobs.append((glob_scores, None, functools.partial(_store_heads, og_ref.at[rs, :], g=g, rows=seq)))
            jobs.append((win_scores, _sink_column(sink_ref, g, seq),
                         functools.partial(_store_heads, ow_ref.at[rs, :], g=g, rows=seq)))
    _attention_jobs(jobs)


def _attn_ctx(sink, qg, kg, vg, qw, kw, vw, seq):
    t_tokens = qg.shape[0]
    rows = 8 * seq
    spec = lambda wd: pl.BlockSpec((rows, wd), lambda i: (i, 0))
    return pl.pallas_call(
        functools.partial(_attn_ctx_kernel, seq=seq),
        grid=(t_tokens // rows,),
        in_specs=[pl.BlockSpec(memory_space=pltpu.SMEM),
                  spec(GLOB_Q), spec(GLOB_KVW), spec(GLOB_KVW), spec(WIN_Q), spec(WIN_KVW), spec(WIN_KVW)],
        out_specs=[spec(GLOB_Q), spec(WIN_Q)],
        out_shape=[jax.ShapeDtypeStruct((t_tokens, GLOB_Q), BF16),
                   jax.ShapeDtypeStruct((t_tokens, WIN_Q), BF16)],
        compiler_params=_params("arbitrary"),
        name="attn_ctx",
    )(sink, qg, kg, vg, qw, kw, vw)


def _attn_lat_kernel(sink_ref, qg_ref, kg_ref, vg_ref, cgk_ref, cgv_ref,
                     qw_ref, kw_ref, vw_ref, cwk_ref, cwv_ref, og_ref, ow_ref,
                     cgk_b, cgv_b, cwk_b, cwv_b, *, seq, qb):
    bi = pl.program_id(1)

    @pl.when(bi == 0)
    def _():
        for src, dst in ((cgk_ref, cgk_b), (cgv_ref, cgv_b), (cwk_ref, cwk_b), (cwv_ref, cwv_b)):
            past = dst.shape[0]
            kv = src.shape[0] // past
            for h in range(kv):
                dst[:, h * HEAD_DIM:(h + 1) * HEAD_DIM] = src[pl.ds(h, past, stride=kv), :].astype(BF16)

    span = QSUB + 2 * WINDOW
    jobs = []
    for sb in range(qb // QSUB):
        rs = slice(sb * QSUB, (sb + 1) * QSUB)
        q0 = bi * qb + sb * QSUB
        start = pl.multiple_of(jnp.clip(q0 - WINDOW, 0, seq - span), HEAD_DIM)
        qpos = q0 + lax.broadcasted_iota(jnp.int32, (QSUB, span), 0)
        kpos = start + lax.broadcasted_iota(jnp.int32, (QSUB, span), 1)
        band = jnp.where(jnp.abs(qpos - kpos) <= WINDOW, 0.0, NEG_INF)
        band = jnp.concatenate([band] * GROUP, axis=0)
        for g in range(GLOB_KV):
            hs = slice(g * HEAD_DIM, (g + 1) * HEAD_DIM)

            def glob_scores(rs=rs, hs=hs, g=g):
                q = _stack_heads(qg_ref[rs, :], g)
                return [(_scores(q, cgk_b[:, hs]), cgv_b[:, hs]), (_scores(q, kg_ref[:, hs]), vg_ref[:, hs])]

            def win_scores(rs=rs, hs=hs, g=g, start=start, band=band):
                q = _stack_heads(qw_ref[rs, :], g)
                s_loc = _scores(q, kw_ref[pl.ds(start, span), hs]) + band
                return [(_scores(q, cwk_b[:, hs]), cwv_b[:, hs]), (s_loc, vw_ref[pl.ds(start, span), hs])]

            jobs.append((glob_scores, None, functools.partial(_store_heads, og_ref.at[rs, :], g=g, rows=QSUB)))
            jobs.append((win_scores, _sink_column(sink_ref, g, QSUB),
                         functools.partial(_store_heads, ow_ref.at[rs, :], g=g, rows=QSUB)))
    _attention_jobs(jobs[0::2] + jobs[1::2])


def _attn_lat(sink, layer, qg, kg, vg, cgk, cgv, qw, kw, vw, cwk, cwv, seq):
    t_tokens = qg.shape[0]
    nbatch = t_tokens // seq
    qb = QSUB
    nqb = seq // qb
    past = cgk.shape[2] // GLOB_KV
    qspec = lambda wd: pl.BlockSpec((qb, wd), lambda b, i: (b * nqb + i, 0))
    kvspec = lambda wd: pl.BlockSpec((seq, wd), lambda b, i: (b, 0))
    cspec = lambda wd: pl.BlockSpec((None, None, past * (wd // HEAD_DIM), HEAD_DIM), lambda b, i: (b, layer, 0, 0))
    return pl.pallas_call(
        functools.partial(_attn_lat_kernel, seq=seq, qb=qb),
        grid=(nbatch, nqb),
        in_specs=[pl.BlockSpec(memory_space=pltpu.SMEM),
                  qspec(GLOB_Q), kvspec(GLOB_KVW), kvspec(GLOB_KVW), cspec(GLOB_KVW), cspec(GLOB_KVW),
                  qspec(WIN_Q), kvspec(WIN_KVW), kvspec(WIN_KVW), cspec(WIN_KVW), cspec(WIN_KVW)],
        out_specs=[qspec(GLOB_Q), qspec(WIN_Q)],
        out_shape=[jax.ShapeDtypeStruct((t_tokens, GLOB_Q), BF16),
                   jax.ShapeDtypeStruct((t_tokens, WIN_Q), BF16)],
        scratch_shapes=[pltpu.VMEM((past, wd), BF16) for wd in (GLOB_KVW, GLOB_KVW, WIN_KVW, WIN_KVW)],
        compiler_params=_params("arbitrary", "arbitrary"),
        name="attn_lat",
    )(sink, qg, kg, vg, cgk, cgv, qw, kw, vw, cwk, cwv)


def _outproj_kernel(x_ref, mod_ref, g_ref, yc_ref, yg_ref, yw_ref, w_ref, o_ref, h_ref, *, rchunk):
    c1 = CONV_CH
    c2 = CONV_CH + GLOB_Q
    for r in range(x_ref.shape[0] // rchunk):
        rs = slice(r * rchunk, (r + 1) * rchunk)
        mix = (jnp.dot(yc_ref[rs, :], w_ref[0:c1, :], preferred_element_type=F32)
               + jnp.dot(yg_ref[rs, :], w_ref[c1:c2, :], preferred_element_type=F32)
               + jnp.dot(yw_ref[rs, :], w_ref[c2:, :], preferred_element_type=F32))
        x1 = x_ref[rs, :] + mod_ref[2:3, :] * mix
        o_ref[rs, :] = x1
        h = _rms(x1, g_ref[...]) * (1.0 + mod_ref[4:5, :]) + mod_ref[3:4, :]
        h_ref[rs, :] = h.astype(BF16)


def _outproj(x, mod, row_of_tile, g_mlp, yc, yg, yw, w):
    t_tokens, d = x.shape
    tm = 512
    row = lambda wd: pl.BlockSpec((tm, wd), lambda i: (i, 0))
    return pl.pallas_call(
        functools.partial(_outproj_kernel, rchunk=256),
        grid=(t_tokens // tm,),
        in_specs=[row(d),
                  pl.BlockSpec((None, 6, d), lambda i: (row_of_tile(i, tm), 0, 0)),
                  pl.BlockSpec((1, d), lambda i: (0, 0)),
                  row(CONV_CH), row(GLOB_Q), row(WIN_Q),
                  pl.BlockSpec(w.shape, lambda i: (0, 0), pipeline_mode=pl.Buffered(1))],
        out_specs=[row(d), row(d)],
        out_shape=[jax.ShapeDtypeStruct((t_tokens, d), F32), jax.ShapeDtypeStruct((t_tokens, d), BF16)],
        compiler_params=_params("arbitrary"),
        name="outproj",
    )(x, mod, g_mlp, yc, yg, yw, w)


def _mlp_kernel(*refs, final_norm, n_cast):
    n_in = 5 + int(final_norm)
    x_ref, h_ref, mod_ref, w1_ref, w2_ref = refs[:5]
    cast_in = refs[n_in:n_in + n_cast]
    o_ref = refs[n_in + n_cast]
    cast_out = refs[n_in + n_cast + 1:]
    acc_ref = o_ref
    j = pl.program_id(1)
    last = pl.num_programs(1) - 1

    def partial_sum():
        a = jnp.maximum(jnp.dot(h_ref[...], w1_ref[...], preferred_element_type=F32), 0.0)
        part = jnp.dot((a * a).astype(BF16), w2_ref[...], preferred_element_type=F32)
        for src, dst in zip(cast_in, cast_out):
            dst[...] = src[...].astype(BF16)
        return part

    @pl.when(j == 0)
    def _():
        acc_ref[...] = partial_sum()

    @pl.when(jnp.logical_and(j > 0, j < last))
    def _():
        acc_ref[...] += partial_sum()

    @pl.when(j == last)
    def _():
        y = x_ref[...] + mod_ref[5:6, :] * (acc_ref[...] + partial_sum())
        if final_norm:
            y = _rms(y, refs[5][...])
        o_ref[...] = y


def _mlp(x, h, mod, row_of_tile, w1, w2, g_final, cast_jobs):
    t_tokens, d = x.shape
    d_ff = w1.shape[1]
    tm, tf = 512, 1024
    ni, nj = t_tokens // tm, d_ff // tf
    assert nj >= 2
    final_norm = g_final is not None
    in_specs = [
        pl.BlockSpec((tm, d), lambda i, j: (i, 0)),
        pl.BlockSpec((tm, d), lambda i, j: (i, 0)),
        pl.BlockSpec((None, 6, d), lambda i, j: (row_of_tile(i, tm), 0, 0)),
        pl.BlockSpec((d, tf), lambda i, j: (0, j)),
        pl.BlockSpec((tf, d), lambda i, j: (j, 0)),
    ]
    args = [x, h, mod, w1, w2]
    if final_norm:
        in_specs.append(pl.BlockSpec((1, d), lambda i, j: (0, 0)))
        args.append(g_final)
    out_specs = [pl.BlockSpec((tm, d), lambda i, j: (i, 0))]
    out_shape = [jax.ShapeDtypeStruct((t_tokens, d), F32)]
    for w, layer in cast_jobs:
        _, rows, cols = w.shape
        rps = rows // (ni * nj)
        assert rps * ni * nj == rows and rps % 16 == 0
        in_specs.append(pl.BlockSpec((None, rps, cols), lambda i, j, layer=layer: (layer, i * nj + j, 0)))
        args.append(w)
        out_specs.append(pl.BlockSpec((rps, cols), lambda i, j: (i * nj + j, 0)))
        out_shape.append(jax.ShapeDtypeStruct((rows, cols), BF16))
    return pl.pallas_call(
        functools.partial(_mlp_kernel, final_norm=final_norm, n_cast=len(cast_jobs)),
        grid=(ni, nj),
        in_specs=in_specs,
        out_specs=out_specs,
        out_shape=out_shape,
        compiler_params=_params("arbitrary", "arbitrary"),
        name="mlp_final" if final_norm else "mlp",
    )(*args)


def _rope_tables(n_tokens):
    rows = n_tokens // GRID_W
    r, col = jnp.meshgrid(jnp.arange(rows), jnp.arange(GRID_W), indexing='ij')
    r = r.reshape(-1).astype(F32)
    col = col.reshape(-1).astype(F32)
    n_freq = HEAD_DIM // 4
    inv = ROPE_THETA ** (-jnp.arange(n_freq, dtype=F32) / n_freq)
    ang_r = r[:, None] * inv
    ang_c = col[:, None] * inv
    ang = jnp.concatenate([ang_r, ang_r, ang_c, ang_c], axis=-1)
    sign = jnp.asarray(np.tile(np.repeat(np.array([-1.0, 1.0], np.float32), n_freq), 2))
    return jnp.cos(ang), jnp.sin(ang) * sign


def kernel(x_prompt, x_sample, cache_glob_k, cache_glob_v, cache_win_k, cache_win_v, c, c_ctx, w_ada, b_ada,
           g_attn, g_mlp, w_in, conv_w, conv_b, conv_ln_g, conv_ln_b, q_norm_g, k_norm_g, sink, w_out,
           w_mlp1, w_mlp2, g_final):
    batch, seq, d = x_prompt.shape
    dec_batch, dec_seq, _ = x_sample.shape
    depth = w_ada.shape[0]
    ctx_row = dec_batch
    assert dec_batch < MOD_ROWS

    cin = jnp.concatenate([c, c_ctx[None, :], jnp.zeros((MOD_ROWS - dec_batch - 1, d), F32)], axis=0)
    mods = _ada(cin, w_ada, b_ada).reshape(depth, MOD_ROWS, 6, d)
    rope_tabs = _rope_tables(dec_seq)

    w_in_b, w_out_b, w1_b, w2_b = (w[0].astype(BF16) for w in (w_in, w_out, w_mlp1, w_mlp2))
    caches = [a.reshape(dec_batch, depth, -1, HEAD_DIM) for a in (cache_glob_k, cache_glob_v, cache_win_k, cache_win_v)]

    ctx_rows = lambda i, tm: ctx_row
    lat_rows = lambda i, tm: (i * tm) // dec_seq

    xp = x_prompt.reshape(batch * seq, d)
    xs = x_sample.reshape(dec_batch * dec_seq, d)
    new_cache = [jnp.zeros((batch, depth, seq * kv, HEAD_DIM), F32) for kv in (GLOB_KV, GLOB_KV, WIN_KV, WIN_KV)]
    for l in range(depth):
        mod = mods[l]
        row = lambda a: a[l].reshape(1, -1)
        g_fin = g_final.reshape(1, d) if l == depth - 1 else None
        conv_args = (conv_w[l], row(conv_b), row(conv_ln_g), row(conv_ln_b))
        more = l + 1 < depth
        cast_ctx = [(w_in, l + 1), (w_out, l + 1), (w_mlp1, l + 1)] if more else []
        cast_lat = [(w_mlp2, l + 1)] if more else []

        u, qg, kg, vg, qw, kw, vw, *new_cache = _inproj(
            xp, mod, ctx_rows, row(g_attn), w_in_b, l, row(q_norm_g), row(k_norm_g), None, seq, new_cache)
        yc = _conv(u, *conv_args, seq)
        yg, yw = _attn_ctx(sink[l], qg, kg, vg, qw, kw, vw, seq)
        xp, hp = _outproj(xp, mod, ctx_rows, row(g_mlp), yc, yg, yw, w_out_b)
        xp, *next_ctx = _mlp(xp, hp, mod, ctx_rows, w1_b, w2_b, g_fin, cast_ctx)

        u, qg, kg, vg, qw, kw, vw = _inproj(
            xs, mod, lat_rows, row(g_attn), w_in_b, l, row(q_norm_g), row(k_norm_g), rope_tabs, dec_seq, None)
        yc = _conv(u, *conv_args, dec_seq)
        yg, yw = _attn_lat(sink[l], l, qg, kg, vg, caches[0], caches[1], qw, kw, vw, caches[2], caches[3],
                           dec_seq)
        xs, hs = _outproj(xs, mod, lat_rows, row(g_mlp), yc, yg, yw, w_out_b)
        xs, *next_lat = _mlp(xs, hs, mod, lat_rows, w1_b, w2_b, g_fin, cast_lat)
        if more:
            (w_in_b, w_out_b, w1_b), (w2_b,) = next_ctx, next_lat

    outs = [a.reshape(batch, depth, seq, -1, HEAD_DIM) for a in new_cache]
    return (xp.reshape(batch, seq, d), xs.reshape(dec_batch, dec_seq, d), *outs)
```

```python
import functools

import jax
import jax.numpy as jnp
import numpy as np
from jax import lax
from jax.experimental import pallas as pl
from jax.experimental.pallas import tpu as pltpu

HEAD_DIM = 128
CONV_CH = 512
CONV_WIDTH = 31
GLOB_HEADS = 6
GLOB_KV = 2
WIN_HEADS = 6
WIN_KV = 2
WINDOW = 128
GRID_W = 64
ROPE_THETA = 10000.0
NORM_EPS = 1e-6
NEG_INF = -1e30

GLOB_Q = GLOB_HEADS * HEAD_DIM
GLOB_KVW = GLOB_KV * HEAD_DIM
WIN_Q = WIN_HEADS * HEAD_DIM
WIN_KVW = WIN_KV * HEAD_DIM
GROUP = GLOB_HEADS // GLOB_KV
C_U = 0
C_QG = 2 * CONV_CH
C_KG = C_QG + GLOB_Q
C_VG = C_KG + GLOB_KVW
C_QW = C_VG + GLOB_KVW
C_KW = C_QW + WIN_Q
C_VW = C_KW + WIN_KVW
IN_WIDTH = C_VW + WIN_KVW

LOG2E = float(np.log2(np.e))
Q_PRESCALE = HEAD_DIM ** -0.5 * LOG2E

MOD_ROWS = 16
CONV_HALO = 16
SUBLANES = 8
CONV_BLOCK = 256
QSUB = 128
SCORE_LOOKAHEAD = 1
V7X_VMEM_LIMIT = 56 * 1024 * 1024

BF16 = jnp.bfloat16
F32 = jnp.float32


def _params(*sem):
    return pltpu.CompilerParams(dimension_semantics=sem, vmem_limit_bytes=V7X_VMEM_LIMIT)


def _sigmoid(x):
    return 1.0 / (1.0 + jnp.exp2(x * -LOG2E))


def _rms(x, gain):
    return x * lax.rsqrt(jnp.mean(x * x, axis=-1, keepdims=True) + NORM_EPS) * gain


def _ada_kernel(c_ref, w_ref, b_ref, o_ref):
    c = c_ref[...]
    a = (c * _sigmoid(c)).astype(BF16)
    o_ref[...] = jnp.dot(a, w_ref[...].astype(BF16), preferred_element_type=F32) + b_ref[...]


def _ada(cin, w_ada, b_ada):
    depth, d, n = w_ada.shape
    tn = 2048
    return pl.pallas_call(
        _ada_kernel,
        grid=(depth, n // tn),
        in_specs=[
            pl.BlockSpec((MOD_ROWS, d), lambda l, j: (0, 0)),
            pl.BlockSpec((None, d, tn), lambda l, j: (l, 0, j)),
            pl.BlockSpec((None, 1, tn), lambda l, j: (l, 0, j)),
        ],
        out_specs=pl.BlockSpec((None, MOD_ROWS, tn), lambda l, j: (l, 0, j)),
        out_shape=jax.ShapeDtypeStruct((depth, MOD_ROWS, n), F32),
        compiler_params=_params("arbitrary", "arbitrary"),
        name="ada_mod",
    )(cin, w_ada, b_ada.reshape(depth, 1, n))


def _store_cache(ref, kv, cols, t):
    nb, rows, _ = ref.shape
    seq = rows // kv
    heads = range(kv)[slice(cols.start and cols.start // HEAD_DIM, cols.stop and cols.stop // HEAD_DIM)]
    for b in range(nb):
        for i, h in enumerate(heads):
            ref[b, pl.ds(h, seq, stride=kv), :] = t[b * seq:(b + 1) * seq, i * HEAD_DIM:(i + 1) * HEAD_DIM]


def _project(x_ref, mod_ref, g_ref, w_ref, qn_ref, kn_ref, rope_refs, outs, caches):
    u_ref, qg_ref, kg_ref, vg_ref, qw_ref, kw_ref, vw_ref = outs
    h = _rms(x_ref[...], g_ref[...]) * (1.0 + mod_ref[1:2, :]) + mod_ref[0:1, :]
    hb = h.astype(BF16)

    def proj(c0, c1):
        return jnp.dot(hb, w_ref[:, c0:c1], preferred_element_type=F32)

    if rope_refs is not None:
        cos = rope_refs[0][...]
        sin = rope_refs[1][...]
        lane = lax.broadcasted_iota(jnp.int32, cos.shape, 1)
        take_next = ((lane // (HEAD_DIM // 4)) % 2) == 0

        def rot(t):
            r = jnp.where(take_next, pltpu.roll(t, HEAD_DIM - HEAD_DIM // 4, 1),
                          pltpu.roll(t, HEAD_DIM // 4, 1))
            return t * cos + r * sin
    else:
        def rot(t):
            return t

    def heads(p, n):
        return [(slice(i * HEAD_DIM, (i + 1) * HEAD_DIM), p[:, i * HEAD_DIM:(i + 1) * HEAD_DIM])
                for i in range(n)]

    qn = qn_ref[...]
    kn = kn_ref[...]
    for cols, t in heads(proj(C_QG, C_KG), GLOB_HEADS):
        qg_ref[:, cols] = (rot(_rms(t, qn)) * Q_PRESCALE).astype(BF16)
    for cols, t in heads(proj(C_KG, C_VG), GLOB_KV):
        t = _rms(t, kn)
        if caches is not None:
            _store_cache(caches[0], GLOB_KV, cols, t)
        kg_ref[:, cols] = rot(t).astype(BF16)
    p = proj(C_VG, C_QW)
    if caches is not None:
        _store_cache(caches[1], GLOB_KV, slice(None), p)
    vg_ref[...] = p.astype(BF16)
    for cols, t in heads(proj(C_QW, C_KW), WIN_HEADS):
        qw_ref[:, cols] = (rot(t) * Q_PRESCALE).astype(BF16)
    p = proj(C_KW, C_VW)
    if caches is not None:
        _store_cache(caches[2], WIN_KV, slice(None), p)
    for cols, t in heads(p, WIN_KV):
        kw_ref[:, cols] = rot(t).astype(BF16)
    p = proj(C_VW, IN_WIDTH)
    if caches is not None:
        _store_cache(caches[3], WIN_KV, slice(None), p)
    vw_ref[...] = p.astype(BF16)
    u_ref[...] = proj(C_U, C_QG)


PROJ_WIDTHS = ((C_QG - C_U, F32), (GLOB_Q, BF16), (GLOB_KVW, BF16), (GLOB_KVW, BF16),
               (WIN_Q, BF16), (WIN_KVW, BF16), (WIN_KVW, BF16))


def _inproj_kernel(*refs, rope, n_aliased):
    x_ref, mod_ref, g_ref, w_ref, qn_ref, kn_ref = refs[:6]
    pos = 6
    rope_refs = None
    if rope:
        rope_refs = refs[pos:pos + 2]
        pos += 2
    pos += n_aliased
    outs = refs[pos:pos + len(PROJ_WIDTHS)]
    caches = refs[pos + len(PROJ_WIDTHS):] or None
    _project(x_ref, mod_ref, g_ref, w_ref, qn_ref, kn_ref, rope_refs, outs, caches)


def _inproj(x, mod, row_of_tile, g, w, layer, qn, kn, rope_tabs, seq, prev_cache):
    t_tokens, d = x.shape
    tm = 512
    rope = rope_tabs is not None
    const = lambda i: (0, 0)
    in_specs = [
        pl.BlockSpec((tm, d), lambda i: (i, 0)),
        pl.BlockSpec((None, 6, d), lambda i: (row_of_tile(i, tm), 0, 0)),
        pl.BlockSpec((1, d), const),
        pl.BlockSpec((d, IN_WIDTH), const, pipeline_mode=pl.Buffered(1)),
        pl.BlockSpec((1, HEAD_DIM), const),
        pl.BlockSpec((1, HEAD_DIM), const),
    ]
    args = [x, mod, g, w, qn, kn]
    if rope:
        per_seq = seq // tm
        in_specs += [pl.BlockSpec((tm, HEAD_DIM), lambda i: (i % per_seq, 0))] * 2
        args += list(rope_tabs)
    out_specs = [pl.BlockSpec((tm, wd), lambda i: (i, 0)) for wd, _ in PROJ_WIDTHS]
    out_shape = [jax.ShapeDtypeStruct((t_tokens, wd), dt) for wd, dt in PROJ_WIDTHS]
    aliases = {}
    n_aliased = 0
    if prev_cache is not None:
        nb = tm // seq
        n_aliased = len(prev_cache)
        for prev in prev_cache:
            aliases[len(args)] = len(out_shape)
            in_specs.append(pl.BlockSpec(memory_space=pl.ANY))
            args.append(prev)
            out_specs.append(pl.BlockSpec((nb, None) + prev.shape[2:], lambda i: (i, layer, 0, 0)))
            out_shape.append(jax.ShapeDtypeStruct(prev.shape, prev.dtype))
    return pl.pallas_call(
        functools.partial(_inproj_kernel, rope=rope, n_aliased=n_aliased),
        grid=(t_tokens // tm,),
        in_specs=in_specs,
        out_specs=out_specs,
        out_shape=out_shape,
        input_output_aliases=aliases,
        compiler_params=_params("arbitrary"),
        name="inproj_rope" if rope else "inproj_ctx",
    )(*args)


def _conv_body(u_ref, halo, w_ref, b_ref, lg_ref, lb_ref, o_ref, hp_ref):
    rows = u_ref.shape[0]

    def glu(u):
        return u[:, :CONV_CH] * _sigmoid(u[:, CONV_CH:])

    zeros = jnp.zeros((CONV_HALO, CONV_CH), F32)
    if halo is not None:
        up_ref, un_ref, has_prev, has_next = halo
        hp_ref[0:CONV_HALO, :] = jnp.where(has_prev, glu(up_ref[...]), zeros)
        hp_ref[CONV_HALO + rows:, :] = jnp.where(has_next, glu(un_ref[...]), zeros)
    else:
        hp_ref[0:CONV_HALO, :] = zeros
        hp_ref[CONV_HALO + rows:, :] = zeros
    hp_ref[CONV_HALO:CONV_HALO + rows, :] = glu(u_ref[...])

    rchunk = 128
    base = CONV_HALO - CONV_WIDTH // 2
    groups = -(-(base + CONV_WIDTH) // SUBLANES)
    cols = []
    for c in range(CONV_CH // HEAD_DIM):
        cs = slice(c * HEAD_DIM, (c + 1) * HEAD_DIM)
        col = []
        for r in range(rows // rchunk):
            acc = None
            for s in range(SUBLANES):
                part = None
                for q in range(groups):
                    k = q * SUBLANES + s - base
                    if 0 <= k < CONV_WIDTH:
                        r0 = r * rchunk + q * SUBLANES
                        term = hp_ref[r0:r0 + rchunk + SUBLANES, cs] * w_ref[k:k + 1, cs]
                        part = term if part is None else part + term
                part = part[s:s + rchunk]
                acc = part if acc is None else acc + part
            col.append(acc)
        cols.append(jnp.concatenate(col, axis=0))

    y = jnp.concatenate(cols, axis=1) + b_ref[...]
    mu = jnp.mean(y, axis=-1, keepdims=True)
    yc = y - mu
    var = jnp.mean(yc * yc, axis=-1, keepdims=True)
    z = yc * lax.rsqrt(var + NORM_EPS) * lg_ref[...] + lb_ref[...]
    o_ref[...] = (z * _sigmoid(z)).astype(BF16)


def _conv_kernel(*refs, blocks_per_seq):
    if blocks_per_seq > 1:
        u_ref, up_ref, un_ref = refs[:3]
        j = pl.program_id(0) % blocks_per_seq
        halo = (up_ref, un_ref, j > 0, j < blocks_per_seq - 1)
        rest = refs[3:]
    else:
        u_ref, halo, rest = refs[0], None, refs[1:]
    _conv_body(u_ref, halo, *rest)


def _conv(u, w, b, lg, lb, seq):
    t_tokens = u.shape[0]
    rows = CONV_BLOCK
    blocks_per_seq = seq // rows
    halo_per_block = rows // CONV_HALO
    n_halo = t_tokens // CONV_HALO
    const = lambda i: (0, 0)
    in_specs = [pl.BlockSpec((rows, 2 * CONV_CH), lambda i: (i, 0))]
    args = [u]
    if blocks_per_seq > 1:
        in_specs += [
            pl.BlockSpec((CONV_HALO, 2 * CONV_CH), lambda i: (jnp.maximum(i * halo_per_block - 1, 0), 0)),
            pl.BlockSpec((CONV_HALO, 2 * CONV_CH),
                         lambda i: (jnp.minimum((i + 1) * halo_per_block, n_halo - 1), 0)),
        ]
        args += [u, u]
    in_specs += [pl.BlockSpec((CONV_WIDTH, CONV_CH), const)] + [pl.BlockSpec((1, CONV_CH), const)] * 3
    args += [w, b, lg, lb]
    return pl.pallas_call(
        functools.partial(_conv_kernel, blocks_per_seq=blocks_per_seq),
        grid=(t_tokens // rows,),
        in_specs=in_specs,
        out_specs=pl.BlockSpec((rows, CONV_CH), lambda i: (i, 0)),
        out_shape=jax.ShapeDtypeStruct((t_tokens, CONV_CH), BF16),
        scratch_shapes=[pltpu.VMEM((rows + 2 * CONV_HALO, CONV_CH), F32)],
        compiler_params=_params("arbitrary"),
        name="conv_mixer",
    )(*args)


def _stack_heads(q, g):
    return jnp.concatenate(
        [q[:, (g * GROUP + j) * HEAD_DIM:(g * GROUP + j + 1) * HEAD_DIM] for j in range(GROUP)], axis=0)


def _scores(q, k):
    return lax.dot_general(q, k, (((1,), (1,)), ((), ())), preferred_element_type=F32)


def _sink_column(sink_ref, g, rows):
    return jnp.concatenate(
        [jnp.full((rows, 1), sink_ref[g * GROUP + j] * LOG2E, F32) for j in range(GROUP)], axis=0)


def _store_heads(o_ref, o, g, rows):
    for j in range(GROUP):
        c0 = (g * GROUP + j) * HEAD_DIM
        o_ref[:, c0:c0 + HEAD_DIM] = o[j * rows:(j + 1) * rows].astype(o_ref.dtype)


def _attention_jobs(jobs):
    n = len(jobs)
    pieces = [None] * n
    for t in range(-SCORE_LOOKAHEAD, n):
        if t + SCORE_LOOKAHEAD < n:
            pieces[t + SCORE_LOOKAHEAD] = jobs[t + SCORE_LOOKAHEAD][0]()
        if t < 0:
            continue
        _, snk, store = jobs[t]
        m = functools.reduce(jnp.maximum, [jnp.max(s, axis=-1, keepdims=True) for s, _ in pieces[t]])
        if snk is not None:
            m = jnp.maximum(m, snk)
        es = [jnp.exp2(s - m) for s, _ in pieces[t]]
        den = functools.reduce(jnp.add, [jnp.sum(e, axis=-1, keepdims=True) for e in es])
        if snk is not None:
            den = den + jnp.exp2(snk - m)
        o = functools.reduce(jnp.add, [jnp.dot(e.astype(BF16), v, preferred_element_type=F32)
                                       for e, (_, v) in zip(es, pieces[t])])
        store(o * (1.0 / den))
        pieces[t] = None


def _attn_ctx_kernel(sink_ref, qg_ref, kg_ref, vg_ref, qw_ref, kw_ref, vw_ref, og_ref, ow_ref, *, seq):
    jobs = []
    for b in range(qg_ref.shape[0] // seq):
        rs = slice(b * seq, (b + 1) * seq)
        for g in range(GLOB_KV):
            hs = slice(g * HEAD_DIM, (g + 1) * HEAD_DIM)

            def glob_scores(rs=rs, hs=hs, g=g):
                return [(_scores(_stack_heads(qg_ref[rs, :], g), kg_ref[rs, hs]), vg_ref[rs, hs])]

            def win_scores(rs=rs, hs=hs, g=g):
                return [(_scores(_stack_heads(qw_ref[rs, :], g), kw_ref[rs, hs]), vw_ref[rs, hs])]

            jobs.append((glob_scores, None, functools.partial(_store_heads, og_ref.at[rs, :], g=g, rows=seq)))
            jobs.append((win_scores, _sink_column(sink_ref, g, seq),
                         functools.partial(_store_heads, ow_ref.at[rs, :], g=g, rows=seq)))
    _attention_jobs(jobs)


def _attn_ctx(sink, qg, kg, vg, qw, kw, vw, seq):
    t_tokens = qg.shape[0]
    rows = 8 * seq
    spec = lambda wd: pl.BlockSpec((rows, wd), lambda i: (i, 0))
    return pl.pallas_call(
        functools.partial(_attn_ctx_kernel, seq=seq),
        grid=(t_tokens // rows,),
        in_specs=[pl.BlockSpec(memory_space=pltpu.SMEM),
                  spec(GLOB_Q), spec(GLOB_KVW), spec(GLOB_KVW), spec(WIN_Q), spec(WIN_KVW), spec(WIN_KVW)],
        out_specs=[spec(GLOB_Q), spec(WIN_Q)],
        out_shape=[jax.ShapeDtypeStruct((t_tokens, GLOB_Q), BF16),
                   jax.ShapeDtypeStruct((t_tokens, WIN_Q), BF16)],
        compiler_params=_params("arbitrary"),
        name="attn_ctx",
    )(sink, qg, kg, vg, qw, kw, vw)


def _attn_lat_kernel(sink_ref, qg_ref, kg_ref, vg_ref, cgk_ref, cgv_ref,
                     qw_ref, kw_ref, vw_ref, cwk_ref, cwv_ref, og_ref, ow_ref,
                     cgk_b, cgv_b, cwk_b, cwv_b, *, seq, qb):
    bi = pl.program_id(1)

    @pl.when(bi == 0)
    def _():
        for src, dst in ((cgk_ref, cgk_b), (cgv_ref, cgv_b), (cwk_ref, cwk_b), (cwv_ref, cwv_b)):
            past = dst.shape[0]
            kv = src.shape[0] // past
            for h in range(kv):
                dst[:, h * HEAD_DIM:(h + 1) * HEAD_DIM] = src[pl.ds(h, past, stride=kv), :].astype(BF16)

    span = QSUB + 2 * WINDOW
    jobs = []
    for sb in range(qb // QSUB):
        rs = slice(sb * QSUB, (sb + 1) * QSUB)
        q0 = bi * qb + sb * QSUB
        start = pl.multiple_of(jnp.clip(q0 - WINDOW, 0, seq - span), HEAD_DIM)
        qpos = q0 + lax.broadcasted_iota(jnp.int32, (QSUB, span), 0)
        kpos = start + lax.broadcasted_iota(jnp.int32, (QSUB, span), 1)
        band = jnp.where(jnp.abs(qpos - kpos) <= WINDOW, 0.0, NEG_INF)
        band = jnp.concatenate([band] * GROUP, axis=0)
        for g in range(GLOB_KV):
            hs = slice(g * HEAD_DIM, (g + 1) * HEAD_DIM)

            def glob_scores(rs=rs, hs=hs, g=g):
                q = _stack_heads(qg_ref[rs, :], g)
                return [(_scores(q, cgk_b[:, hs]), cgv_b[:, hs]), (_scores(q, kg_ref[:, hs]), vg_ref[:, hs])]

            def win_scores(rs=rs, hs=hs, g=g, start=start, band=band):
                q = _stack_heads(qw_ref[rs, :], g)
                s_loc = _scores(q, kw_ref[pl.ds(start, span), hs]) + band
                return [(_scores(q, cwk_b[:, hs]), cwv_b[:, hs]), (s_loc, vw_ref[pl.ds(start, span), hs])]

            jobs.append((glob_scores, None, functools.partial(_store_heads, og_ref.at[rs, :], g=g, rows=QSUB)))
            jobs.append((win_scores, _sink_column(sink_ref, g, QSUB),
                         functools.partial(_store_heads, ow_ref.at[rs, :], g=g, rows=QSUB)))
    _attention_jobs(jobs[0::2] + jobs[1::2])


def _attn_lat(sink, layer, qg, kg, vg, cgk, cgv, qw, kw, vw, cwk, cwv, seq):
    t_tokens = qg.shape[0]
    nbatch = t_tokens // seq
    qb = QSUB
    nqb = seq // qb
    past = cgk.shape[2] // GLOB_KV
    qspec = lambda wd: pl.BlockSpec((qb, wd), lambda b, i: (b * nqb + i, 0))
    kvspec = lambda wd: pl.BlockSpec((seq, wd), lambda b, i: (b, 0))
    cspec = lambda wd: pl.BlockSpec((None, None, past * (wd // HEAD_DIM), HEAD_DIM), lambda b, i: (b, layer, 0, 0))
    return pl.pallas_call(
        functools.partial(_attn_lat_kernel, seq=seq, qb=qb),
        grid=(nbatch, nqb),
        in_specs=[pl.BlockSpec(memory_space=pltpu.SMEM),
                  qspec(GLOB_Q), kvspec(GLOB_KVW), kvspec(GLOB_KVW), cspec(GLOB_KVW), cspec(GLOB_KVW),
                  qspec(WIN_Q), kvspec(WIN_KVW), kvspec(WIN_KVW), cspec(WIN_KVW), cspec(WIN_KVW)],
        out_specs=[qspec(GLOB_Q), qspec(WIN_Q)],
        out_shape=[jax.ShapeDtypeStruct((t_tokens, GLOB_Q), BF16),
                   jax.ShapeDtypeStruct((t_tokens, WIN_Q), BF16)],
        scratch_shapes=[pltpu.VMEM((past, wd), BF16) for wd in (GLOB_KVW, GLOB_KVW, WIN_KVW, WIN_KVW)],
        compiler_params=_params("arbitrary", "arbitrary"),
        name="attn_lat",
    )(sink, qg, kg, vg, cgk, cgv, qw, kw, vw, cwk, cwv)


def _outproj_kernel(x_ref, mod_ref, g_ref, yc_ref, yg_ref, yw_ref, w_ref, o_ref, h_ref, *, rchunk):
    c1 = CONV_CH
    c2 = CONV_CH + GLOB_Q
    for r in range(x_ref.shape[0] // rchunk):
        rs = slice(r * rchunk, (r + 1) * rchunk)
        mix = (jnp.dot(yc_ref[rs, :], w_ref[0:c1, :], preferred_element_type=F32)
               + jnp.dot(yg_ref[rs, :], w_ref[c1:c2, :], preferred_element_type=F32)
               + jnp.dot(yw_ref[rs, :], w_ref[c2:, :], preferred_element_type=F32))
        x1 = x_ref[rs, :] + mod_ref[2:3, :] * mix
        o_ref[rs, :] = x1
        h = _rms(x1, g_ref[...]) * (1.0 + mod_ref[4:5, :]) + mod_ref[3:4, :]
        h_ref[rs, :] = h.astype(BF16)


def _outproj(x, mod, row_of_tile, g_mlp, yc, yg, yw, w):
    t_tokens, d = x.shape
    tm = 512
    row = lambda wd: pl.BlockSpec((tm, wd), lambda i: (i, 0))
    return pl.pallas_call(
        functools.partial(_outproj_kernel, rchunk=256),
        grid=(t_tokens // tm,),
        in_specs=[row(d),
                  pl.BlockSpec((None, 6, d), lambda i: (row_of_tile(i, tm), 0, 0)),
                  pl.BlockSpec((1, d), lambda i: (0, 0)),
                  row(CONV_CH), row(GLOB_Q), row(WIN_Q),
                  pl.BlockSpec(w.shape, lambda i: (0, 0), pipeline_mode=pl.Buffered(1))],
        out_specs=[row(d), row(d)],
        out_shape=[jax.ShapeDtypeStruct((t_tokens, d), F32), jax.ShapeDtypeStruct((t_tokens, d), BF16)],
        compiler_params=_params("arbitrary"),
        name="outproj",
    )(x, mod, g_mlp, yc, yg, yw, w)


def _mlp_kernel(*refs, final_norm, n_cast):
    n_in = 5 + int(final_norm)
    x_ref, h_ref, mod_ref, w1_ref, w2_ref = refs[:5]
    cast_in = refs[n_in:n_in + n_cast]
    o_ref = refs[n_in + n_cast]
    cast_out = refs[n_in + n_cast + 1:]
    acc_ref = o_ref
    j = pl.program_id(1)
    last = pl.num_programs(1) - 1

    def partial_sum():
        a = jnp.maximum(jnp.dot(h_ref[...], w1_ref[...], preferred_element_type=F32), 0.0)
        part = jnp.dot((a * a).astype(BF16), w2_ref[...], preferred_element_type=F32)
        for src, dst in zip(cast_in, cast_out):
            dst[...] = src[...].astype(BF16)
        return part

    xs_ref = refs[-1]
    nchunk = xs_ref.shape[0] + 1

    @pl.when(j == 0)
    def _():
        acc_ref[...] = partial_sum()
        xs_ref[0] = x_ref[...]

    @pl.when(jnp.logical_and(j > 0, j < last))
    def _():
        acc_ref[...] += partial_sum()
        xs_ref[j] = x_ref[...]

    @pl.when(j == last)
    def _():
        x_full = jnp.concatenate([xs_ref[c] for c in range(nchunk - 1)] + [x_ref[...]], axis=1)
        y = x_full + mod_ref[5:6, :] * (acc_ref[...] + partial_sum())
        if final_norm:
            y = _rms(y, refs[5][...])
        o_ref[...] = y


def _mlp(x, h, mod, row_of_tile, w1, w2, g_final, cast_jobs):
    t_tokens, d = x.shape
    d_ff = w1.shape[1]
    tm, tf = 512, 1024
    ni, nj = t_tokens // tm, d_ff // tf
    assert nj >= 2
    final_norm = g_final is not None
    in_specs = [
        pl.BlockSpec((tm, d // nj), lambda i, j: (i, j)),
        pl.BlockSpec((tm, d), lambda i, j: (i, 0)),
        pl.BlockSpec((None, 6, d), lambda i, j: (row_of_tile(i, tm), 0, 0)),
        pl.BlockSpec((d, tf), lambda i, j: (0, j)),
        pl.BlockSpec((tf, d), lambda i, j: (j, 0)),
    ]
    args = [x, h, mod, w1, w2]
    if final_norm:
        in_specs.append(pl.BlockSpec((1, d), lambda i, j: (0, 0)))
        args.append(g_final)
    out_specs = [pl.BlockSpec((tm, d), lambda i, j: (i, 0))]
    out_shape = [jax.ShapeDtypeStruct((t_tokens, d), F32)]
    for w, layer in cast_jobs:
        _, rows, cols = w.shape
        rps = rows // (ni * nj)
        assert rps * ni * nj == rows and rps % 16 == 0
        in_specs.append(pl.BlockSpec((None, rps, cols), lambda i, j, layer=layer: (layer, i * nj + j, 0)))
        args.append(w)
        out_specs.append(pl.BlockSpec((rps, cols), lambda i, j: (i * nj + j, 0)))
        out_shape.append(jax.ShapeDtypeStruct((rows, cols), BF16))
    return pl.pallas_call(
        functools.partial(_mlp_kernel, final_norm=final_norm, n_cast=len(cast_jobs)),
        grid=(ni, nj),
        in_specs=in_specs,
        out_specs=out_specs,
        out_shape=out_shape,
        scratch_shapes=[pltpu.VMEM((nj - 1, tm, d // nj), F32)],
        compiler_params=_params("arbitrary", "arbitrary"),
        name="mlp_final" if final_norm else "mlp",
    )(*args)


def _rope_tables(n_tokens):
    rows = n_tokens // GRID_W
    r, col = jnp.meshgrid(jnp.arange(rows), jnp.arange(GRID_W), indexing='ij')
    r = r.reshape(-1).astype(F32)
    col = col.reshape(-1).astype(F32)
    n_freq = HEAD_DIM // 4
    inv = ROPE_THETA ** (-jnp.arange(n_freq, dtype=F32) / n_freq)
    ang_r = r[:, None] * inv
    ang_c = col[:, None] * inv
    ang = jnp.concatenate([ang_r, ang_r, ang_c, ang_c], axis=-1)
    sign = jnp.asarray(np.tile(np.repeat(np.array([-1.0, 1.0], np.float32), n_freq), 2))
    return jnp.cos(ang), jnp.sin(ang) * sign


def kernel(x_prompt, x_sample, cache_glob_k, cache_glob_v, cache_win_k, cache_win_v, c, c_ctx, w_ada, b_ada,
           g_attn, g_mlp, w_in, conv_w, conv_b, conv_ln_g, conv_ln_b, q_norm_g, k_norm_g, sink, w_out,
           w_mlp1, w_mlp2, g_final):
    batch, seq, d = x_prompt.shape
    dec_batch, dec_seq, _ = x_sample.shape
    depth = w_ada.shape[0]
    ctx_row = dec_batch
    assert dec_batch < MOD_ROWS

    cin = jnp.concatenate([c, c_ctx[None, :], jnp.zeros((MOD_ROWS - dec_batch - 1, d), F32)], axis=0)
    mods = _ada(cin, w_ada, b_ada).reshape(depth, MOD_ROWS, 6, d)
    rope_tabs = _rope_tables(dec_seq)

    w_in_b, w_out_b, w1_b, w2_b = (w[0].astype(BF16) for w in (w_in, w_out, w_mlp1, w_mlp2))
    caches = [a.reshape(dec_batch, depth, -1, HEAD_DIM) for a in (cache_glob_k, cache_glob_v, cache_win_k, cache_win_v)]

    ctx_rows = lambda i, tm: ctx_row
    lat_rows = lambda i, tm: (i * tm) // dec_seq

    xp = x_prompt.reshape(batch * seq, d)
    xs = x_sample.reshape(dec_batch * dec_seq, d)
    new_cache = [jnp.zeros((batch, depth, seq * kv, HEAD_DIM), F32) for kv in (GLOB_KV, GLOB_KV, WIN_KV, WIN_KV)]
    for l in range(depth):
        mod = mods[l]
        row = lambda a: a[l].reshape(1, -1)
        g_fin = g_final.reshape(1, d) if l == depth - 1 else None
        conv_args = (conv_w[l], row(conv_b), row(conv_ln_g), row(conv_ln_b))
        more = l + 1 < depth
        cast_ctx = [(w_in, l + 1), (w_out, l + 1), (w_mlp1, l + 1)] if more else []
        cast_lat = [(w_mlp2, l + 1)] if more else []

        u, qg, kg, vg, qw, kw, vw, *new_cache = _inproj(
            xp, mod, ctx_rows, row(g_attn), w_in_b, l, row(q_norm_g), row(k_norm_g), None, seq, new_cache)
        yc = _conv(u, *conv_args, seq)
        yg, yw = _attn_ctx(sink[l], qg, kg, vg, qw, kw, vw, seq)
        xp, hp = _outproj(xp, mod, ctx_rows, row(g_mlp), yc, yg, yw, w_out_b)
        xp, *next_ctx = _mlp(xp, hp, mod, ctx_rows, w1_b, w2_b, g_fin, cast_ctx)

        u, qg, kg, vg, qw, kw, vw = _inproj(
            xs, mod, lat_rows, row(g_attn), w_in_b, l, row(q_norm_g), row(k_norm_g), rope_tabs, dec_seq, None)
        yc = _conv(u, *conv_args, dec_seq)
        yg, yw = _attn_lat(sink[l], l, qg, kg, vg, caches[0], caches[1], qw, kw, vw, caches[2], caches[3],
                           dec_seq)
        xs, hs = _outproj(xs, mod, lat_rows, row(g_mlp), yc, yg, yw, w_out_b)
        xs, *next_lat = _mlp(xs, hs, mod, lat_rows, w1_b, w2_b, g_fin, cast_lat)
        if more:
            (w_in_b, w_out_b, w1_b), (w2_b,) = next_ctx, next_lat

    outs = [a.reshape(batch, depth, seq, -1, HEAD_DIM) for a in new_cache]
    return (xp.reshape(batch, seq, d), xs.reshape(dec_batch, dec_seq, d), *outs)
```
